```python
import math
import jax, jax.numpy as jnp
from jax import lax
import numpy as np

D_MODEL = 1024
BATCH = 16
SEQ = 4096
DEPTH = 1

ATTN_WIDTH = D_MODEL // 2
RNN_WIDTH = D_MODEL - ATTN_WIDTH
ATTN_HEADS = 4
ATTN_HALF_DIM = ATTN_WIDTH // (2 * ATTN_HEADS)
ATTN_V_DIM = 2 * ATTN_HALF_DIM
RNN_HEADS = 4
RNN_HEAD_DIM = RNN_WIDTH // RNN_HEADS
D_FF = 2816
N_BUCKETS = 32
MAX_DISTANCE = 128
Q_BLOCK = 128
CHUNK = 64
EPS = 1e-6
IN_COLS = 3 * ATTN_WIDTH + 5 * RNN_WIDTH

kernel_name = "hymba_diffattn_hgrn2_macaron_encoder"


def rmsnorm(x, w):
    xf = x.astype(jnp.float32)
    y = xf * lax.rsqrt(jnp.mean(xf * xf, axis=-1, keepdims=True) + EPS)
    return (y * w.astype(jnp.float32)).astype(x.dtype)


def rel_bucket(rel):
    nb = N_BUCKETS // 2
    max_exact = nb // 2
    side = jnp.where(rel > 0, nb, 0)
    n = jnp.abs(rel)
    nf = jnp.maximum(n, 1).astype(jnp.float32)
    large = max_exact + (jnp.log(nf / max_exact) / math.log(MAX_DISTANCE / max_exact)
                         * (nb - max_exact)).astype(jnp.int32)
    large = jnp.minimum(large, nb - 1)
    return side + jnp.where(n < max_exact, n, large)


def diff_attention(q, k, v, rel_bias, lam):
    B, S = q.shape[0], q.shape[1]
    nblk = S // Q_BLOCK
    qb = jnp.moveaxis(q.reshape(B, nblk, Q_BLOCK, ATTN_HEADS, 2, ATTN_HALF_DIM), 1, 0)
    starts = jnp.arange(nblk, dtype=jnp.int32) * Q_BLOCK
    k_pos = jnp.arange(S, dtype=jnp.int32)
    scale = ATTN_HALF_DIM ** -0.5

    def block(args):
        q_blk, start = args
        q_pos = start + jnp.arange(Q_BLOCK, dtype=jnp.int32)
        bias = jnp.take(rel_bias, rel_bucket(k_pos[None, :] - q_pos[:, None]), axis=0)
        bias = jnp.transpose(bias, (2, 0, 1)).astype(jnp.float32)
        s = jnp.einsum('bqhcd,bkhcd->bhcqk', q_blk, k).astype(jnp.float32) * scale + bias[None, :, None]
        p = jax.nn.softmax(s, axis=-1)
        w = (p[:, :, 0] - lam * p[:, :, 1]).astype(v.dtype)
        return jnp.einsum('bhqk,bkhe->bqhe', w, v)

    o = lax.map(block, (qb, starts))
    return jnp.moveaxis(o, 0, 1).reshape(B, S, ATTN_HEADS, ATTN_V_DIM)


def hgrn2_chunk_scan(q, k, v, g):
    B, S, H, dk = q.shape
    dv = v.shape[-1]
    n = S // CHUNK

    def chunks(t):
        return jnp.moveaxis(t.reshape(B, n, CHUNK, H, t.shape[-1]), 1, 0)

    qc, kc, vc, gc = chunks(q), chunks(k), chunks(v), chunks(g)
    G = jnp.cumsum(gc, axis=2)
    G_last = G[:, :, -1]
    G_mid = G[:, :, CHUNK // 2 - 1:CHUNK // 2]
    q_intra = qc * jnp.exp(G - G_mid)
    k_intra = kc * jnp.exp(G_mid - G)
    lower = jnp.tril(jnp.ones((CHUNK, CHUNK), dtype=bool))
    A = jnp.einsum('nbthk,nbshk->nbhts', q_intra, k_intra)
    A = jnp.where(lower, A, 0.0)
    o_intra = jnp.einsum('nbhts,nbshv->nbthv', A, vc)
    k_dec = kc * jnp.exp(G_last[:, :, None] - G)

    def step(state, xs):
        kd, vv, gl = xs
        new = jnp.exp(gl)[..., None] * state + jnp.einsum('bchk,bchv->bhkv', kd, vv).astype(jnp.float32)
        return new, state

    s0 = jnp.zeros((B, H, dk, dv), jnp.float32)
    _, s_prev = lax.scan(step, s0, (k_dec, vc, G_last))
    o_inter = jnp.einsum('nbthk,nbhkv->nbthv', qc * jnp.exp(G), s_prev)
    o = o_intra + o_inter
    return jnp.moveaxis(o, 0, 1).reshape(B, S, H, dv)


def hgrn2_gates(f_logits, lb):
    sig = jax.nn.sigmoid(f_logits.astype(jnp.float32))
    log_f = jnp.log(lb + (1.0 - lb) * sig)
    one_minus_f = (1.0 - lb) * (1.0 - sig)
    return log_f, one_minus_f


def swiglu_half_step(h, pre, w_in, w_out, post):
    u = rmsnorm(h, pre)
    gate, up = jnp.split(u @ w_in, 2, axis=-1)
    y = (jax.nn.silu(gate) * up) @ w_out
    return h + 0.5 * rmsnorm(y, post)


def setup_inputs(seed: int = 0) -> dict:
    key = jax.random.key(seed)
    ks = jax.random.split(key, 24)
    f32 = jnp.float32

    def w(k, shape, fan_in):
        return jax.random.normal(k, shape, f32) * fan_in ** -0.5

    def gain(k, shape):
        return 1.0 + 0.02 * jax.random.normal(k, shape, f32)

    L = DEPTH
    return {
        "x": jax.random.normal(ks[0], (BATCH, SEQ, D_MODEL), f32),
        "rel_bias": 0.1 * jax.random.normal(ks[1], (N_BUCKETS, ATTN_HEADS), f32),
        "lb_logits": 0.1 * jax.random.normal(ks[2], (2, DEPTH + 1, RNN_WIDTH), f32),
        "ffn1_pre_norm": gain(ks[3], (L, D_MODEL)),
        "ffn1_w_in": w(ks[4], (L, D_MODEL, 2 * D_FF), D_MODEL),
        "ffn1_w_out": w(ks[5], (L, D_FF, D_MODEL), D_FF),
        "ffn1_post_norm": gain(ks[6], (L, D_MODEL)),
        "mix_pre_norm": gain(ks[7], (L, D_MODEL)),
        "w_mix_in": w(ks[8], (L, D_MODEL, IN_COLS), D_MODEL),
        "lambda_q1": 0.1 * jax.random.normal(ks[9], (L, ATTN_HALF_DIM), f32),
        "lambda_k1": 0.1 * jax.random.normal(ks[10], (L, ATTN_HALF_DIM), f32),
        "lambda_q2": 0.1 * jax.random.normal(ks[11], (L, ATTN_HALF_DIM), f32),
        "lambda_k2": 0.1 * jax.random.normal(ks[12], (L, ATTN_HALF_DIM), f32),
        "attn_head_norm": gain(ks[13], (L, ATTN_V_DIM)),
        "rnn_head_norm": gain(ks[14], (L, RNN_HEAD_DIM)),
        "w_mix_out": w(ks[15], (L, ATTN_WIDTH + RNN_WIDTH, D_MODEL), ATTN_WIDTH + RNN_WIDTH),
        "mix_post_norm": gain(ks[16], (L, D_MODEL)),
        "ffn2_pre_norm": gain(ks[17], (L, D_MODEL)),
        "ffn2_w_in": w(ks[18], (L, D_MODEL, 2 * D_FF), D_MODEL),
        "ffn2_w_out": w(ks[19], (L, D_FF, D_MODEL), D_FF),
        "ffn2_post_norm": gain(ks[20], (L, D_MODEL)),
    }


def reference(x, rel_bias, lb_logits, ffn1_pre_norm, ffn1_w_in, ffn1_w_out, ffn1_post_norm,
              mix_pre_norm, w_mix_in, lambda_q1, lambda_k1, lambda_q2, lambda_k2,
              attn_head_norm, rnn_head_norm, w_mix_out, mix_post_norm,
              ffn2_pre_norm, ffn2_w_in, ffn2_w_out, ffn2_post_norm):
    B, S, _ = x.shape
    splits = [ATTN_WIDTH, 2 * ATTN_WIDTH, 3 * ATTN_WIDTH]
    splits += [3 * ATTN_WIDTH + j * RNN_WIDTH for j in range(1, 5)]
    lb_all = jnp.cumsum(jax.nn.softmax(lb_logits.astype(jnp.float32), axis=1), axis=1)
    h = x
    for layer in range(DEPTH):
        h = swiglu_half_step(h, ffn1_pre_norm[layer], ffn1_w_in[layer], ffn1_w_out[layer],
                             ffn1_post_norm[layer])

        u = rmsnorm(h, mix_pre_norm[layer])
        proj = u @ w_mix_in[layer]
        q_a, k_a, v_a, q_r, i_r, f_fw, f_bw, g_r = jnp.split(proj, splits, axis=-1)

        lambda_init = 0.8 - 0.6 * math.exp(-0.3 * layer)
        lam = (jnp.exp(jnp.sum(lambda_q1[layer] * lambda_k1[layer]).astype(jnp.float32))
               - jnp.exp(jnp.sum(lambda_q2[layer] * lambda_k2[layer]).astype(jnp.float32))
               + lambda_init)
        qa = q_a.reshape(B, S, ATTN_HEADS, 2, ATTN_HALF_DIM)
        ka = k_a.reshape(B, S, ATTN_HEADS, 2, ATTN_HALF_DIM)
        va = v_a.reshape(B, S, ATTN_HEADS, ATTN_V_DIM)
        o_a = diff_attention(qa, ka, va, rel_bias, lam)
        o_a = rmsnorm(o_a, attn_head_norm[layer]) * (1.0 - lambda_init)
        o_a = o_a.reshape(B, S, ATTN_WIDTH)

        lb = lb_all[:, layer]
        g_fw, k_fw = hgrn2_gates(f_fw, lb[0])
        g_bw, k_bw = hgrn2_gates(f_bw, lb[1])
        heads = lambda t: t.reshape(B, S, RNN_HEADS, RNN_HEAD_DIM)
        qr = heads(jax.nn.silu(q_r))
        ir = heads(i_r)
        o_fw = hgrn2_chunk_scan(qr, heads(k_fw), ir, heads(g_fw))
        flip = lambda t: jnp.flip(t, axis=1)
        o_bw = flip(hgrn2_chunk_scan(flip(qr), flip(heads(k_bw)), flip(ir), flip(heads(g_bw))))
        o_r = rmsnorm(o_fw + o_bw, rnn_head_norm[layer]) * jax.nn.silu(heads(g_r))
        o_r = o_r.reshape(B, S, RNN_WIDTH)

        mixed = jnp.concatenate([o_a, o_r.astype(o_a.dtype)], axis=-1) @ w_mix_out[layer]
        h = h + rmsnorm(mixed, mix_post_norm[layer])

        h = swiglu_half_step(h, ffn2_pre_norm[layer], ffn2_w_in[layer], ffn2_w_out[layer],
                             ffn2_post_norm[layer])
    return h.astype(x.dtype)
```

```python
import functools
import math

import numpy as np
import jax
import jax.numpy as jnp
from jax import lax
from jax.experimental import pallas as pl
from jax.experimental.pallas import tpu as pltpu

F32 = jnp.float32
BF16 = jnp.bfloat16

D_MODEL = 1024
ATTN_WIDTH = 512
RNN_WIDTH = 512
ATTN_HEADS = 4
ATTN_HALF_DIM = 64
HEAD_DIM = 128
RNN_HEADS = 4
D_FF = 2816
N_BUCKETS = 32
MAX_DISTANCE = 128
CHUNK = 64
EPS = 1e-6
LAMBDA_INIT = 0.8 - 0.6 * math.exp(-0.3 * 0)

VMEM_LIMIT_BYTES = 56 * 1024 * 1024
LANES = 128

TOKEN_TILE = 512
FF_CHUNK = 256
Q_TILE = 256
BIAS_HALO = 128
RNN_GROUP = 256

NT_DIMS = (((1,), (1,)), ((), ()))
TN_DIMS = (((0,), (0,)), ((), ()))


def _rms(x, w):
    return x * lax.rsqrt(jnp.mean(x * x, axis=-1, keepdims=True) + EPS) * w


def _dot(a, b):
    return jnp.dot(a, b, preferred_element_type=F32)


def _resident(shape):
    return pl.BlockSpec(shape, lambda *_: (0,) * len(shape), pipeline_mode=pl.Buffered(1))


def _params(*semantics):
    return pltpu.CompilerParams(dimension_semantics=semantics, vmem_limit_bytes=VMEM_LIMIT_BYTES)


def _swiglu_half_step(h, pre_ref, w_in_ref, w_out_ref, post_ref, acc_ref):
    u = _rms(h, pre_ref[...]).astype(BF16)
    for c in range(D_FF // FF_CHUNK):
        lo = c * FF_CHUNK
        gate = _dot(u, w_in_ref[:, lo:lo + FF_CHUNK])
        up = _dot(u, w_in_ref[:, D_FF + lo:D_FF + lo + FF_CHUNK])
        act = (gate * jax.nn.sigmoid(gate) * up).astype(BF16)
        part = _dot(act, w_out_ref[lo:lo + FF_CHUNK, :])
        if c == 0:
            acc_ref[...] = part
        else:
            acc_ref[...] += part
    return h + 0.5 * _rms(acc_ref[...], post_ref[...])


def _ffn_body(x_ref, pre_ref, w_in_ref, w_out_ref, post_ref, o_ref, acc_ref):
    o_ref[...] = _swiglu_half_step(x_ref[...], pre_ref, w_in_ref, w_out_ref, post_ref, acc_ref)


def _ffn(x, pre, w_in, w_out, post):
    n, d = x.shape
    tile = pl.BlockSpec((TOKEN_TILE, d), lambda i: (i, 0))
    return pl.pallas_call(
        _ffn_body,
        grid=(n // TOKEN_TILE,),
        in_specs=[tile, _resident(pre.shape), _resident(w_in.shape), _resident(w_out.shape),
                  _resident(post.shape)],
        out_specs=tile,
        out_shape=jax.ShapeDtypeStruct((n, d), F32),
        scratch_shapes=[pltpu.VMEM((TOKEN_TILE, d), F32)],
        compiler_params=_params("parallel"),
        name="ffn",
    )(x, pre, w_in, w_out, post)


def _mix_ffn_body(h_ref, oa_ref, or_ref, w_mix_ref, mix_post_ref, pre_ref, w_in_ref, w_out_ref,
                  post_ref, o_ref, acc_ref):
    mixed = (_dot(oa_ref[...], w_mix_ref[:ATTN_WIDTH, :])
             + _dot(or_ref[...], w_mix_ref[ATTN_WIDTH:, :]))
    h = h_ref[...] + _rms(mixed, mix_post_ref[...])
    o_ref[...] = _swiglu_half_step(h, pre_ref, w_in_ref, w_out_ref, post_ref, acc_ref)


def _mix_ffn(h, o_a, o_r, w_mix, mix_post, pre, w_in, w_out, post):
    n, d = h.shape
    tile = pl.BlockSpec((TOKEN_TILE, d), lambda i: (i, 0))
    half = pl.BlockSpec((TOKEN_TILE, ATTN_WIDTH), lambda i: (i, 0))
    return pl.pallas_call(
        _mix_ffn_body,
        grid=(n // TOKEN_TILE,),
        in_specs=[tile, half, half, _resident(w_mix.shape), _resident(mix_post.shape),
                  _resident(pre.shape), _resident(w_in.shape), _resident(w_out.shape),
                  _resident(post.shape)],
        out_specs=tile,
        out_shape=jax.ShapeDtypeStruct((n, d), F32),
        scratch_shapes=[pltpu.VMEM((TOKEN_TILE, d), F32)],
        compiler_params=_params("parallel"),
        name="mix_ffn",
    )(h, o_a, o_r, w_mix, mix_post, pre, w_in, w_out, post)


N_PROJ = 8
PROJ_WIDTH = 512
N_ATTN_PROJ = 3


def _mixproj_body(h_ref, norm_ref, w_ref, *out_refs):
    u = _rms(h_ref[...], norm_ref[...]).astype(BF16)
    for j, o_ref in enumerate(out_refs):
        p = _dot(u, w_ref[:, j * PROJ_WIDTH:(j + 1) * PROJ_WIDTH])
        if j == 0:
            p = p * (ATTN_HALF_DIM ** -0.5)
        o_ref[...] = p.astype(o_ref.dtype)


def _mixproj(h, norm, w):
    n, d = h.shape
    tile = pl.BlockSpec((TOKEN_TILE, d), lambda i: (i, 0))
    out_tile = pl.BlockSpec((TOKEN_TILE, PROJ_WIDTH), lambda i: (i, 0))
    dtypes = [BF16] * N_ATTN_PROJ + [F32] * (N_PROJ - N_ATTN_PROJ)
    return pl.pallas_call(
        _mixproj_body,
        grid=(n // TOKEN_TILE,),
        in_specs=[tile, _resident(norm.shape), _resident(w.shape)],
        out_specs=[out_tile] * N_PROJ,
        out_shape=[jax.ShapeDtypeStruct((n, PROJ_WIDTH), dt) for dt in dtypes],
        compiler_params=_params("parallel"),
        name="mixproj",
    )(h, norm, w)


def _rel_bucket(rel):
    nb = N_BUCKETS // 2
    max_exact = nb // 2
    side = jnp.where(rel > 0, nb, 0)
    n = jnp.abs(rel)
    nf = jnp.maximum(n, 1).astype(jnp.float32)
    large = max_exact + (jnp.log(nf / max_exact) / math.log(MAX_DISTANCE / max_exact)
                         * (nb - max_exact)).astype(jnp.int32)
    large = jnp.minimum(large, nb - 1)
    return side + jnp.where(n < max_exact, n, large)


def _bias_tables(rel_bias):
    nb = N_BUCKETS // 2
    i = jnp.arange(Q_TILE, dtype=jnp.int32)[:, None]
    j = jnp.arange(Q_TILE + 2 * BIAS_HALO, dtype=jnp.int32)[None, :]
    band = jnp.take(rel_bias.astype(F32), _rel_bucket(j - BIAS_HALO - i), axis=0)
    band = jnp.transpose(band, (2, 0, 1))
    far = jnp.stack([rel_bias[nb - 1], rel_bias[2 * nb - 1]], axis=-1).astype(F32)
    added = jnp.where(j < BIAS_HALO, far[:, None, None, 0], far[:, None, None, 1])
    return band - added, far


def _attn_body(far_ref, lam_ref, q_ref, k_ref, v_ref, band_ref, norm_ref, o_ref, s_ref, p_ref):
    h = pl.program_id(1)
    qi = pl.program_id(2)
    n_q = pl.num_programs(2)
    seq = k_ref.shape[1]
    n_kc = seq // Q_TILE
    n_pieces = (Q_TILE + 2 * BIAS_HALO) // LANES
    q = q_ref[0]
    lane = lax.broadcasted_iota(jnp.int32, q.shape, 1)
    far_before = far_ref[h, 0]
    far_after = far_ref[h, 1]

    outs = []
    for c in range(2):
        in_half = (lane < ATTN_HALF_DIM) if c == 0 else (lane >= ATTN_HALF_DIM)
        qc = jnp.where(in_half, q, jnp.zeros_like(q))

        def scores(kc, carry):
            start = pl.multiple_of(kc * Q_TILE, Q_TILE)
            s = lax.dot_general(qc, k_ref[0, pl.ds(start, Q_TILE), :], NT_DIMS,
                                preferred_element_type=F32)
            s_ref[:, pl.ds(start, Q_TILE)] = s + jnp.where(kc < qi, far_before, far_after)
            return carry
        lax.fori_loop(0, n_kc, scores, 0)

        for piece in range(n_pieces):
            def fix(piece=piece):
                start = pl.multiple_of(qi * Q_TILE + (piece * LANES - BIAS_HALO), LANES)
                s_ref[:, pl.ds(start, LANES)] += band_ref[0, :, piece * LANES:(piece + 1) * LANES]
            if piece == 0:
                pl.when(qi > 0)(fix)
            elif piece == n_pieces - 1:
                pl.when(qi < n_q - 1)(fix)
            else:
                fix()

        def row_max(kc, m):
            start = pl.multiple_of(kc * Q_TILE, Q_TILE)
            return jnp.maximum(m, jnp.max(s_ref[:, pl.ds(start, Q_TILE)], axis=-1, keepdims=True))
        m = lax.fori_loop(0, n_kc, row_max, jnp.full((Q_TILE, 1), -jnp.inf, F32))

        def probs(kc, l):
            start = pl.multiple_of(kc * Q_TILE, Q_TILE)
            p = jnp.exp(s_ref[:, pl.ds(start, Q_TILE)] - m)
            p_ref[:, pl.ds(start, Q_TILE)] = p.astype(BF16)
            return l + jnp.sum(p, axis=-1, keepdims=True)
        l = lax.fori_loop(0, n_kc, probs, jnp.zeros((Q_TILE, 1), F32))

        outs.append(_dot(p_ref[...], v_ref[0]) / l)

    o = outs[0] - lam_ref[0] * outs[1]
    o_ref[0] = (_rms(o, norm_ref[...]) * (1.0 - LAMBDA_INIT)).astype(o_ref.dtype)


def _attention(q, k, v, band, far, lam, head_norm):
    b, s, _ = q.shape
    smem = pl.BlockSpec(memory_space=pltpu.SMEM)
    return pl.pallas_call(
        _attn_body,
        grid=(b, ATTN_HEADS, s // Q_TILE),
        in_specs=[smem, smem,
                  pl.BlockSpec((1, Q_TILE, HEAD_DIM), lambda bi, h, i: (bi, i, h)),
                  pl.BlockSpec((1, s, HEAD_DIM), lambda bi, h, i: (bi, 0, h)),
                  pl.BlockSpec((1, s, HEAD_DIM), lambda bi, h, i: (bi, 0, h)),
                  pl.BlockSpec((1, Q_TILE, Q_TILE + 2 * BIAS_HALO), lambda bi, h, i: (h, 0, 0)),
                  _resident(head_norm.shape)],
        out_specs=pl.BlockSpec((1, Q_TILE, HEAD_DIM), lambda bi, h, i: (bi, i, h)),
        out_shape=jax.ShapeDtypeStruct((b, s, ATTN_WIDTH), BF16),
        scratch_shapes=[pltpu.VMEM((Q_TILE, s), F32), pltpu.VMEM((Q_TILE, s), BF16)],
        compiler_params=_params("parallel", "parallel", "parallel"),
        name="attn",
    )(far, lam, q, k, v, band, head_norm)


def _chunk_triangles():
    t = np.arange(RNN_GROUP)[:, None]
    s = np.arange(RNN_GROUP)[None, :]
    same = (t // CHUNK) == (s // CHUNK)
    return np.stack([same & (s <= t), same & (s >= t)]).astype(np.float32)


def _hgrn2_body(q_ref, v_ref, ff_ref, fb_ref, gate_ref, lb_ref, tri_ref, norm_ref, o_ref,
                u_ref, dl_ref, qg_ref, acc_ref):
    seq = q_ref.shape[1]
    n_groups = seq // RNN_GROUP
    n_chunks = seq // CHUNK
    per_group = RNN_GROUP // CHUNK

    def chunked(t):
        return t.reshape(per_group, CHUNK, HEAD_DIM)

    def flat(t):
        return t.reshape(RNN_GROUP, HEAD_DIM)

    def intra(direction, f_ref, first):
        mid = CHUNK // 2 - 1 if direction == 0 else CHUNK // 2
        last = CHUNK - 1 if direction == 0 else 0
        lb = lb_ref[direction]
        tri = tri_ref[direction]

        def group(gi, carry):
            rows = pl.ds(pl.multiple_of(gi * RNN_GROUP, RNN_GROUP), RNN_GROUP)
            sig = jax.nn.sigmoid(f_ref[0, rows, :])
            log_f = jnp.log(lb + (1.0 - lb) * sig)
            k = (1.0 - lb) * (1.0 - sig)
            qs = q_ref[0, rows, :]
            qs = qs * jax.nn.sigmoid(qs)
            v = v_ref[0, rows, :].astype(BF16)
            hi = log_f.astype(BF16)
            lo = (log_f - hi.astype(F32)).astype(BF16)
            cum = chunked(_dot(tri, hi) + _dot(tri, lo))
            cum_mid = cum[:, mid:mid + 1, :]
            cum_last = cum[:, last:last + 1, :]
            q_in = flat(chunked(qs) * jnp.exp(cum - cum_mid)).astype(BF16)
            k_in = flat(chunked(k) * jnp.exp(cum_mid - cum)).astype(BF16)
            a = lax.dot_general(q_in, k_in, NT_DIMS, preferred_element_type=F32)
            a = jnp.where(tri > 0, a, 0.0).astype(BF16)
            o_intra = _dot(a, v)
            if first:
                acc_ref[rows, :] = o_intra
            else:
                acc_ref[rows, :] += o_intra
            qg_ref[rows, :] = flat(chunked(qs) * jnp.exp(cum)).astype(BF16)
            k_dec = (chunked(k) * jnp.exp(cum_last - cum)).astype(BF16)
            decay = jnp.exp(cum_last)
            for j in range(per_group):
                ci = gi * per_group + j
                u_ref[ci] = lax.dot_general(v[j * CHUNK:(j + 1) * CHUNK], k_dec[j], TN_DIMS,
                                            preferred_element_type=F32)
                dl_ref[ci] = decay[j]
            return carry
        lax.fori_loop(0, n_groups, group, 0)

    def inter(direction):
        def step(i, state):
            ci = i if direction == 0 else n_chunks - 1 - i
            rows = pl.ds(pl.multiple_of(ci * CHUNK, CHUNK), CHUNK)
            acc_ref[rows, :] += lax.dot_general(qg_ref[rows, :], state.astype(BF16), NT_DIMS,
                                                preferred_element_type=F32)
            return state * dl_ref[ci] + u_ref[ci]
        lax.fori_loop(0, n_chunks, step, jnp.zeros((HEAD_DIM, HEAD_DIM), F32))

    intra(0, ff_ref, True)
    inter(0)
    intra(1, fb_ref, False)
    inter(1)

    def finish(gi, carry):
        rows = pl.ds(pl.multiple_of(gi * RNN_GROUP, RNN_GROUP), RNN_GROUP)
        g = gate_ref[0, rows, :]
        o = _rms(acc_ref[rows, :], norm_ref[...]) * (g * jax.nn.sigmoid(g))
        o_ref[0, rows, :] = o.astype(o_ref.dtype)
        return carry
    lax.fori_loop(0, n_groups, finish, 0)


def _hgrn2(q, v, f_fw, f_bw, gate, lb, head_norm):
    b, s, _ = q.shape
    n_chunks = s // CHUNK
    tri = jnp.asarray(_chunk_triangles(), BF16)
    seq_block = pl.BlockSpec((1, s, HEAD_DIM), lambda bi, h: (bi, 0, h))
    return pl.pallas_call(
        _hgrn2_body,
        grid=(b, RNN_HEADS),
        in_specs=[seq_block] * 5
                 + [pl.BlockSpec((2, 1, HEAD_DIM), lambda bi, h: (0, 0, h)),
                    _resident(tri.shape), _resident(head_norm.shape)],
        out_specs=seq_block,
        out_shape=jax.ShapeDtypeStruct((b, s, RNN_WIDTH), BF16),
        scratch_shapes=[pltpu.VMEM((n_chunks, HEAD_DIM, HEAD_DIM), F32),
                        pltpu.VMEM((n_chunks, 1, HEAD_DIM), F32),
                        pltpu.VMEM((s, HEAD_DIM), BF16),
                        pltpu.VMEM((s, HEAD_DIM), F32)],
        compiler_params=_params("parallel", "parallel"),
        name="hgrn2",
    )(q, v, f_fw, f_bw, gate, lb, tri, head_norm)


def kernel(x, rel_bias, lb_logits, ffn1_pre_norm, ffn1_w_in, ffn1_w_out, ffn1_post_norm, mix_pre_norm, w_mix_in, lambda_q1, lambda_k1, lambda_q2, lambda_k2, attn_head_norm, rnn_head_norm, w_mix_out, mix_post_norm, ffn2_pre_norm, ffn2_w_in, ffn2_w_out, ffn2_post_norm):
    b, s, d = x.shape
    layer = 0
    row = lambda t: t[layer].reshape(1, -1).astype(F32)

    h1 = _ffn(x.reshape(b * s, d), row(ffn1_pre_norm), ffn1_w_in[layer].astype(BF16),
              ffn1_w_out[layer].astype(BF16), row(ffn1_post_norm))

    q_a, k_a, v_a, q_r, i_r, f_fw, f_bw, g_r = _mixproj(h1, row(mix_pre_norm),
                                                       w_mix_in[layer].astype(BF16))
    seq = lambda t: t.reshape(b, s, t.shape[-1])

    lam = (jnp.exp(jnp.sum(lambda_q1[layer] * lambda_k1[layer]).astype(F32))
           - jnp.exp(jnp.sum(lambda_q2[layer] * lambda_k2[layer]).astype(F32))
           + LAMBDA_INIT).reshape(1)
    band, far = _bias_tables(rel_bias)
    o_a = _attention(seq(q_a), seq(k_a), seq(v_a), band, far, lam, row(attn_head_norm))

    lb = jnp.cumsum(jax.nn.softmax(lb_logits.astype(F32), axis=1), axis=1)[:, layer]
    o_r = _hgrn2(seq(q_r), seq(i_r), seq(f_fw), seq(f_bw), seq(g_r),
                 lb.reshape(2, 1, RNN_WIDTH), row(rnn_head_norm))

    y = _mix_ffn(h1, o_a.reshape(b * s, ATTN_WIDTH), o_r.reshape(b * s, RNN_WIDTH),
                 w_mix_out[layer].astype(BF16), row(mix_post_norm), row(ffn2_pre_norm),
                 ffn2_w_in[layer].astype(BF16), ffn2_w_out[layer].astype(BF16),
                 row(ffn2_post_norm))
    return y.reshape(b, s, d).astype(x.dtype)
```

```python
import functools
import math

import numpy as np
import jax
import jax.numpy as jnp
from jax import lax
from jax.experimental import pallas as pl
from jax.experimental.pallas import tpu as pltpu

F32 = jnp.float32
BF16 = jnp.bfloat16

D_MODEL = 1024
ATTN_WIDTH = 512
RNN_WIDTH = 512
ATTN_HEADS = 4
ATTN_HALF_DIM = 64
HEAD_DIM = 128
RNN_HEADS = 4
D_FF = 2816
N_BUCKETS = 32
MAX_DISTANCE = 128
CHUNK = 64
EPS = 1e-6
LAMBDA_INIT = 0.8 - 0.6 * math.exp(-0.3 * 0)

VMEM_LIMIT_BYTES = 56 * 1024 * 1024
LANES = 128

TOKEN_TILE = 512
FF_CHUNK = 256
Q_TILE = 256
KEY_CHUNK = 256
SOFTMAX_ROWS = 64
BIAS_HALO = 128
LOG2E = math.log2(math.e)
RNN_GROUP = 256

NT_DIMS = (((1,), (1,)), ((), ()))
TN_DIMS = (((0,), (0,)), ((), ()))


def _rms(x, w):
    return x * lax.rsqrt(jnp.mean(x * x, axis=-1, keepdims=True) + EPS) * w


def _dot(a, b):
    return jnp.dot(a, b, preferred_element_type=F32)


def _resident(shape):
    return pl.BlockSpec(shape, lambda *_: (0,) * len(shape), pipeline_mode=pl.Buffered(1))


def _params(*semantics):
    return pltpu.CompilerParams(dimension_semantics=semantics, vmem_limit_bytes=VMEM_LIMIT_BYTES)


def _swiglu_half_step(h, pre_ref, w_in_ref, w_out_ref, post_ref, acc_ref):
    u = _rms(h, pre_ref[...]).astype(BF16)
    for c in range(D_FF // FF_CHUNK):
        lo = c * FF_CHUNK
        gate = _dot(u, w_in_ref[:, lo:lo + FF_CHUNK])
        up = _dot(u, w_in_ref[:, D_FF + lo:D_FF + lo + FF_CHUNK])
        act = (gate * jax.nn.sigmoid(gate) * up).astype(BF16)
        part = _dot(act, w_out_ref[lo:lo + FF_CHUNK, :])
        if c == 0:
            acc_ref[...] = part
        else:
            acc_ref[...] += part
    return h + 0.5 * _rms(acc_ref[...], post_ref[...])


def _ffn_body(x_ref, pre_ref, w_in_ref, w_out_ref, post_ref, o_ref, acc_ref):
    o_ref[...] = _swiglu_half_step(x_ref[...], pre_ref, w_in_ref, w_out_ref, post_ref, acc_ref)


def _ffn(x, pre, w_in, w_out, post):
    n, d = x.shape
    tile = pl.BlockSpec((TOKEN_TILE, d), lambda i: (i, 0))
    return pl.pallas_call(
        _ffn_body,
        grid=(n // TOKEN_TILE,),
        in_specs=[tile, _resident(pre.shape), _resident(w_in.shape), _resident(w_out.shape),
                  _resident(post.shape)],
        out_specs=tile,
        out_shape=jax.ShapeDtypeStruct((n, d), F32),
        scratch_shapes=[pltpu.VMEM((TOKEN_TILE, d), F32)],
        compiler_params=_params("parallel"),
        name="ffn",
    )(x, pre, w_in, w_out, post)


def _mix_ffn_body(h_ref, oa_ref, or_ref, w_mix_ref, mix_post_ref, pre_ref, w_in_ref, w_out_ref,
                  post_ref, o_ref, acc_ref):
    mixed = (_dot(oa_ref[...], w_mix_ref[:ATTN_WIDTH, :])
             + _dot(or_ref[...], w_mix_ref[ATTN_WIDTH:, :]))
    h = h_ref[...] + _rms(mixed, mix_post_ref[...])
    o_ref[...] = _swiglu_half_step(h, pre_ref, w_in_ref, w_out_ref, post_ref, acc_ref)


def _mix_ffn(h, o_a, o_r, w_mix, mix_post, pre, w_in, w_out, post):
    n, d = h.shape
    tile = pl.BlockSpec((TOKEN_TILE, d), lambda i: (i, 0))
    half = pl.BlockSpec((TOKEN_TILE, ATTN_WIDTH), lambda i: (i, 0))
    return pl.pallas_call(
        _mix_ffn_body,
        grid=(n // TOKEN_TILE,),
        in_specs=[tile, half, half, _resident(w_mix.shape), _resident(mix_post.shape),
                  _resident(pre.shape), _resident(w_in.shape), _resident(w_out.shape),
                  _resident(post.shape)],
        out_specs=tile,
        out_shape=jax.ShapeDtypeStruct((n, d), F32),
        scratch_shapes=[pltpu.VMEM((TOKEN_TILE, d), F32)],
        compiler_params=_params("parallel"),
        name="mix_ffn",
    )(h, o_a, o_r, w_mix, mix_post, pre, w_in, w_out, post)


N_PROJ = 8
PROJ_WIDTH = 512
N_ATTN_PROJ = 3


def _mixproj_body(h_ref, norm_ref, w_ref, *out_refs):
    u = _rms(h_ref[...], norm_ref[...]).astype(BF16)
    for j, o_ref in enumerate(out_refs):
        p = _dot(u, w_ref[:, j * PROJ_WIDTH:(j + 1) * PROJ_WIDTH])
        if j == 0:
            p = p * (ATTN_HALF_DIM ** -0.5 * LOG2E)
        o_ref[...] = p.astype(o_ref.dtype)


def _mixproj(h, norm, w):
    n, d = h.shape
    tile = pl.BlockSpec((TOKEN_TILE, d), lambda i: (i, 0))
    out_tile = pl.BlockSpec((TOKEN_TILE, PROJ_WIDTH), lambda i: (i, 0))
    dtypes = [BF16] * N_ATTN_PROJ + [F32] * (N_PROJ - N_ATTN_PROJ)
    return pl.pallas_call(
        _mixproj_body,
        grid=(n // TOKEN_TILE,),
        in_specs=[tile, _resident(norm.shape), _resident(w.shape)],
        out_specs=[out_tile] * N_PROJ,
        out_shape=[jax.ShapeDtypeStruct((n, PROJ_WIDTH), dt) for dt in dtypes],
        compiler_params=_params("parallel"),
        name="mixproj",
    )(h, norm, w)


def _rel_bucket(rel):
    nb = N_BUCKETS // 2
    max_exact = nb // 2
    side = jnp.where(rel > 0, nb, 0)
    n = jnp.abs(rel)
    nf = jnp.maximum(n, 1).astype(jnp.float32)
    large = max_exact + (jnp.log(nf / max_exact) / math.log(MAX_DISTANCE / max_exact)
                         * (nb - max_exact)).astype(jnp.int32)
    large = jnp.minimum(large, nb - 1)
    return side + jnp.where(n < max_exact, n, large)


def _bias_tables(rel_bias, seq):
    nb = N_BUCKETS // 2
    n_q = seq // Q_TILE
    assert 2 * n_q <= HEAD_DIM and BIAS_HALO >= MAX_DISTANCE
    rb = rel_bias.astype(F32) * LOG2E
    i = jnp.arange(Q_TILE, dtype=jnp.int32)[:, None]
    j = jnp.arange(Q_TILE + 2 * BIAS_HALO, dtype=jnp.int32)[None, :]
    bucket = _rel_bucket(j - BIAS_HALO - i)
    band = jnp.zeros((ATTN_HEADS, Q_TILE, Q_TILE + 2 * BIAS_HALO), F32)
    for n in range(N_BUCKETS):
        band = jnp.where(bucket[None] == n, rb[n][:, None, None], band)
    before = rb[nb - 1][:, None, None]
    after = rb[2 * nb - 1][:, None, None]
    near = band - jnp.where(j[None] < BIAS_HALO, before, after)
    key_pos = jnp.arange(seq, dtype=jnp.int32)[None, :, None]
    tile_start = (jnp.arange(n_q, dtype=jnp.int32) * Q_TILE)[None, None, :]
    far = jnp.where(key_pos < tile_start, before, after)
    hi = far.astype(BF16)
    lo = (far - hi.astype(F32)).astype(BF16)
    pad = jnp.zeros((ATTN_HEADS, seq, HEAD_DIM - 2 * n_q), BF16)
    return jnp.concatenate([hi, lo, pad], axis=-1), near


def _attn_body(lam_ref, q_ref, k_ref, v_ref, far_ref, near_ref, norm_ref, o_ref,
               kx_ref, vx_ref, qx_ref, s_ref, p_ref):
    qi = pl.program_id(2)
    n_q = pl.num_programs(2)
    seq = k_ref.shape[1]

    @pl.when(qi == 0)
    def _():
        s_ref[:, :BIAS_HALO] = jnp.zeros((2 * Q_TILE, BIAS_HALO), F32)
        s_ref[:, BIAS_HALO + seq:] = jnp.zeros((2 * Q_TILE, BIAS_HALO), F32)
        kx_ref[:, :HEAD_DIM] = k_ref[0]
        kx_ref[:, HEAD_DIM:] = far_ref[0]
        vx_ref[:, :HEAD_DIM] = v_ref[0]
        vx_ref[:, HEAD_DIM:] = jnp.ones((seq, HEAD_DIM), BF16)

    q = q_ref[0]
    lane = lax.broadcasted_iota(jnp.int32, q.shape, 1)
    zero = jnp.zeros_like(q)
    tile_onehot = jnp.where((lane == qi) | (lane == n_q + qi), 1.0, 0.0).astype(BF16)
    qx_ref[:Q_TILE, :HEAD_DIM] = jnp.where(lane < ATTN_HALF_DIM, q, zero)
    qx_ref[Q_TILE:, :HEAD_DIM] = jnp.where(lane >= ATTN_HALF_DIM, q, zero)
    qx_ref[:Q_TILE, HEAD_DIM:] = tile_onehot
    qx_ref[Q_TILE:, HEAD_DIM:] = tile_onehot

    keys = slice(BIAS_HALO, BIAS_HALO + seq)
    window = pl.ds(pl.multiple_of(qi * Q_TILE, Q_TILE), Q_TILE + 2 * BIAS_HALO)
    for c in range(2):
        rows = slice(c * Q_TILE, (c + 1) * Q_TILE)
        s_ref[rows, keys] = lax.dot_general(qx_ref[rows, :], kx_ref[...], NT_DIMS,
                                            preferred_element_type=F32)
        s_ref[rows, window] += near_ref[0]

    outs = []
    for c in range(2):
        for rb in range(Q_TILE // SOFTMAX_ROWS):
            rows = slice(c * Q_TILE + rb * SOFTMAX_ROWS, c * Q_TILE + (rb + 1) * SOFTMAX_ROWS)
            m = jnp.max(s_ref[rows, keys], axis=-1, keepdims=True)
            p_ref[rows, :] = jnp.exp2((s_ref[rows, keys] - m).astype(BF16))
        rows = slice(c * Q_TILE, (c + 1) * Q_TILE)
        acc = _dot(p_ref[rows, :], vx_ref[...])
        outs.append(acc[:, :HEAD_DIM] / acc[:, HEAD_DIM:])
    o = outs[0] - lam_ref[0] * outs[1]
    o_ref[0] = (_rms(o, norm_ref[...]) * (1.0 - LAMBDA_INIT)).astype(o_ref.dtype)


def _attention(q, k, v, far, near, lam, head_norm):
    b, s, _ = q.shape
    smem = pl.BlockSpec(memory_space=pltpu.SMEM)
    return pl.pallas_call(
        _attn_body,
        grid=(b, ATTN_HEADS, s // Q_TILE),
        in_specs=[smem,
                  pl.BlockSpec((1, Q_TILE, HEAD_DIM), lambda bi, h, i: (bi, i, h)),
                  pl.BlockSpec((1, s, HEAD_DIM), lambda bi, h, i: (bi, 0, h)),
                  pl.BlockSpec((1, s, HEAD_DIM), lambda bi, h, i: (bi, 0, h)),
                  pl.BlockSpec((1, s, HEAD_DIM), lambda bi, h, i: (h, 0, 0)),
                  pl.BlockSpec((1, Q_TILE, Q_TILE + 2 * BIAS_HALO), lambda bi, h, i: (h, 0, 0)),
                  _resident(head_norm.shape)],
        out_specs=pl.BlockSpec((1, Q_TILE, HEAD_DIM), lambda bi, h, i: (bi, i, h)),
        out_shape=jax.ShapeDtypeStruct((b, s, ATTN_WIDTH), BF16),
        scratch_shapes=[pltpu.VMEM((s, 2 * HEAD_DIM), BF16), pltpu.VMEM((s, 2 * HEAD_DIM), BF16),
                        pltpu.VMEM((2 * Q_TILE, 2 * HEAD_DIM), BF16),
                        pltpu.VMEM((2 * Q_TILE, s + 2 * BIAS_HALO), F32),
                        pltpu.VMEM((2 * Q_TILE, s), BF16)],
        compiler_params=_params("parallel", "parallel", "arbitrary"),
        name="attn",
    )(lam, q, k, v, far, near, head_norm)


def _chunk_triangles():
    t = np.arange(RNN_GROUP)[:, None]
    s = np.arange(RNN_GROUP)[None, :]
    same = (t // CHUNK) == (s // CHUNK)
    return np.stack([same & (s <= t), same & (s >= t)]).astype(np.float32)


def _hgrn2_body(q_ref, v_ref, ff_ref, fb_ref, gate_ref, lb_ref, tri_ref, norm_ref, o_ref,
                u_ref, dl_ref, qg_ref, acc_ref):
    seq = q_ref.shape[1]
    n_groups = seq // RNN_GROUP
    n_chunks = seq // CHUNK
    per_group = RNN_GROUP // CHUNK

    def chunked(t):
        return t.reshape(per_group, CHUNK, HEAD_DIM)

    def flat(t):
        return t.reshape(RNN_GROUP, HEAD_DIM)

    def intra(direction, f_ref, first):
        mid = CHUNK // 2 - 1 if direction == 0 else CHUNK // 2
        last = CHUNK - 1 if direction == 0 else 0
        lb = lb_ref[direction]
        tri = tri_ref[direction]

        def group(gi, carry):
            rows = pl.ds(pl.multiple_of(gi * RNN_GROUP, RNN_GROUP), RNN_GROUP)
            sig = jax.nn.sigmoid(f_ref[0, rows, :])
            log_f = jnp.log(lb + (1.0 - lb) * sig)
            k = (1.0 - lb) * (1.0 - sig)
            qs = q_ref[0, rows, :]
            qs = qs * jax.nn.sigmoid(qs)
            v = v_ref[0, rows, :].astype(BF16)
            hi = log_f.astype(BF16)
            lo = (log_f - hi.astype(F32)).astype(BF16)
            cum = chunked(_dot(tri, hi) + _dot(tri, lo))
            cum_mid = cum[:, mid:mid + 1, :]
            cum_last = cum[:, last:last + 1, :]
            q_in = flat(chunked(qs) * jnp.exp(cum - cum_mid)).astype(BF16)
            k_in = flat(chunked(k) * jnp.exp(cum_mid - cum)).astype(BF16)
            a = lax.dot_general(q_in, k_in, NT_DIMS, preferred_element_type=F32)
            a = jnp.where(tri > 0, a, 0.0).astype(BF16)
            o_intra = _dot(a, v)
            if first:
                acc_ref[rows, :] = o_intra
            else:
                acc_ref[rows, :] += o_intra
            qg_ref[rows, :] = flat(chunked(qs) * jnp.exp(cum)).astype(BF16)
            k_dec = (chunked(k) * jnp.exp(cum_last - cum)).astype(BF16)
            decay = jnp.exp(cum_last)
            for j in range(per_group):
                ci = gi * per_group + j
                u_ref[ci] = lax.dot_general(v[j * CHUNK:(j + 1) * CHUNK], k_dec[j], TN_DIMS,
                                            preferred_element_type=F32)
                dl_ref[ci] = decay[j]
            return carry
        lax.fori_loop(0, n_groups, group, 0)

    def inter(direction):
        def step(i, state):
            ci = i if direction == 0 else n_chunks - 1 - i
            rows = pl.ds(pl.multiple_of(ci * CHUNK, CHUNK), CHUNK)
            acc_ref[rows, :] += lax.dot_general(qg_ref[rows, :], state.astype(BF16), NT_DIMS,
                                                preferred_element_type=F32)
            return state * dl_ref[ci] + u_ref[ci]
        lax.fori_loop(0, n_chunks, step, jnp.zeros((HEAD_DIM, HEAD_DIM), F32))

    intra(0, ff_ref, True)
    inter(0)
    intra(1, fb_ref, False)
    inter(1)

    def finish(gi, carry):
        rows = pl.ds(pl.multiple_of(gi * RNN_GROUP, RNN_GROUP), RNN_GROUP)
        g = gate_ref[0, rows, :]
        o = _rms(acc_ref[rows, :], norm_ref[...]) * (g * jax.nn.sigmoid(g))
        o_ref[0, rows, :] = o.astype(o_ref.dtype)
        return carry
    lax.fori_loop(0, n_groups, finish, 0)


def _hgrn2(q, v, f_fw, f_bw, gate, lb, head_norm):
    b, s, _ = q.shape
    n_chunks = s // CHUNK
    tri = jnp.asarray(_chunk_triangles(), BF16)
    seq_block = pl.BlockSpec((1, s, HEAD_DIM), lambda bi, h: (bi, 0, h))
    return pl.pallas_call(
        _hgrn2_body,
        grid=(b, RNN_HEADS),
        in_specs=[seq_block] * 5
                 + [pl.BlockSpec((2, 1, HEAD_DIM), lambda bi, h: (0, 0, h)),
                    _resident(tri.shape), _resident(head_norm.shape)],
        out_specs=seq_block,
        out_shape=jax.ShapeDtypeStruct((b, s, RNN_WIDTH), BF16),
        scratch_shapes=[pltpu.VMEM((n_chunks, HEAD_DIM, HEAD_DIM), F32),
                        pltpu.VMEM((n_chunks, 1, HEAD_DIM), F32),
                        pltpu.VMEM((s, HEAD_DIM), BF16),
                        pltpu.VMEM((s, HEAD_DIM), F32)],
        compiler_params=_params("parallel", "parallel"),
        name="hgrn2",
    )(q, v, f_fw, f_bw, gate, lb, tri, head_norm)


def kernel(x, rel_bias, lb_logits, ffn1_pre_norm, ffn1_w_in, ffn1_w_out, ffn1_post_norm, mix_pre_norm, w_mix_in, lambda_q1, lambda_k1, lambda_q2, lambda_k2, attn_head_norm, rnn_head_norm, w_mix_out, mix_post_norm, ffn2_pre_norm, ffn2_w_in, ffn2_w_out, ffn2_post_norm):
    b, s, d = x.shape
    layer = 0
    row = lambda t: t[layer].reshape(1, -1).astype(F32)

    h1 = _ffn(x.reshape(b * s, d), row(ffn1_pre_norm), ffn1_w_in[layer].astype(BF16),
              ffn1_w_out[layer].astype(BF16), row(ffn1_post_norm))

    q_a, k_a, v_a, q_r, i_r, f_fw, f_bw, g_r = _mixproj(h1, row(mix_pre_norm),
                                                       w_mix_in[layer].astype(BF16))
    seq = lambda t: t.reshape(b, s, t.shape[-1])

    lam = (jnp.exp(jnp.sum(lambda_q1[layer] * lambda_k1[layer]).astype(F32))
           - jnp.exp(jnp.sum(lambda_q2[layer] * lambda_k2[layer]).astype(F32))
           + LAMBDA_INIT).reshape(1)
    far, near = _bias_tables(rel_bias, s)
    o_a = _attention(seq(q_a), seq(k_a), seq(v_a), far, near, lam, row(attn_head_norm))

    lb = jnp.cumsum(jax.nn.softmax(lb_logits.astype(F32), axis=1), axis=1)[:, layer]
    o_r = _hgrn2(seq(q_r), seq(i_r), seq(f_fw), seq(f_bw), seq(g_r),
                 lb.reshape(2, 1, RNN_WIDTH), row(rnn_head_norm))

    y = _mix_ffn(h1, o_a.reshape(b * s, ATTN_WIDTH), o_r.reshape(b * s, RNN_WIDTH),
                 w_mix_out[layer].astype(BF16), row(mix_post_norm), row(ffn2_pre_norm),
                 ffn2_w_in[layer].astype(BF16), ffn2_w_out[layer].astype(BF16),
                 row(ffn2_post_norm))
    return y.reshape(b, s, d).astype(x.dtype)
```

```python
import functools
import math

import numpy as np
import jax
import jax.numpy as jnp
from jax import lax
from jax.experimental import pallas as pl
from jax.experimental.pallas import tpu as pltpu

F32 = jnp.float32
BF16 = jnp.bfloat16

D_MODEL = 1024
ATTN_WIDTH = 512
RNN_WIDTH = 512
ATTN_HEADS = 4
ATTN_HALF_DIM = 64
HEAD_DIM = 128
RNN_HEADS = 4
D_FF = 2816
N_BUCKETS = 32
MAX_DISTANCE = 128
CHUNK = 64
EPS = 1e-6
LAMBDA_INIT = 0.8 - 0.6 * math.exp(-0.3 * 0)

VMEM_LIMIT_BYTES = 56 * 1024 * 1024
LANES = 128

TOKEN_TILE = 512
FF_CHUNK = 256
Q_TILE = 256
KEY_CHUNK = 256
SOFTMAX_ROWS = 64
BIAS_HALO = 128
LOG2E = math.log2(math.e)
RNN_GROUP = 256

NT_DIMS = (((1,), (1,)), ((), ()))
TN_DIMS = (((0,), (0,)), ((), ()))


def _rms(x, w):
    return x * lax.rsqrt(jnp.mean(x * x, axis=-1, keepdims=True) + EPS) * w


def _dot(a, b):
    return jnp.dot(a, b, preferred_element_type=F32)


def _resident(shape):
    return pl.BlockSpec(shape, lambda *_: (0,) * len(shape), pipeline_mode=pl.Buffered(1))


def _params(*semantics):
    return pltpu.CompilerParams(dimension_semantics=semantics, vmem_limit_bytes=VMEM_LIMIT_BYTES)


def _swiglu_half_step(h, pre_ref, w_in_ref, w_out_ref, post_ref, acc_ref):
    u = _rms(h, pre_ref[...]).astype(BF16)
    for c in range(D_FF // FF_CHUNK):
        lo = c * FF_CHUNK
        gate = _dot(u, w_in_ref[:, lo:lo + FF_CHUNK])
        up = _dot(u, w_in_ref[:, D_FF + lo:D_FF + lo + FF_CHUNK])
        act = (gate * jax.nn.sigmoid(gate) * up).astype(BF16)
        part = _dot(act, w_out_ref[lo:lo + FF_CHUNK, :])
        if c == 0:
            acc_ref[...] = part
        else:
            acc_ref[...] += part
    return h + 0.5 * _rms(acc_ref[...], post_ref[...])


def _ffn_body(x_ref, pre_ref, w_in_ref, w_out_ref, post_ref, o_ref, acc_ref):
    o_ref[...] = _swiglu_half_step(x_ref[...], pre_ref, w_in_ref, w_out_ref, post_ref, acc_ref)


def _ffn(x, pre, w_in, w_out, post):
    n, d = x.shape
    tile = pl.BlockSpec((TOKEN_TILE, d), lambda i: (i, 0))
    return pl.pallas_call(
        _ffn_body,
        grid=(n // TOKEN_TILE,),
        in_specs=[tile, _resident(pre.shape), _resident(w_in.shape), _resident(w_out.shape),
                  _resident(post.shape)],
        out_specs=tile,
        out_shape=jax.ShapeDtypeStruct((n, d), F32),
        scratch_shapes=[pltpu.VMEM((TOKEN_TILE, d), F32)],
        compiler_params=_params("parallel"),
        name="ffn",
    )(x, pre, w_in, w_out, post)


def _mix_ffn_body(h_ref, oa_ref, or_ref, w_mix_ref, mix_post_ref, pre_ref, w_in_ref, w_out_ref,
                  post_ref, o_ref, acc_ref):
    mixed = (_dot(oa_ref[...], w_mix_ref[:ATTN_WIDTH, :])
             + _dot(or_ref[...], w_mix_ref[ATTN_WIDTH:, :]))
    h = h_ref[...] + _rms(mixed, mix_post_ref[...])
    o_ref[...] = _swiglu_half_step(h, pre_ref, w_in_ref, w_out_ref, post_ref, acc_ref)


def _mix_ffn(h, o_a, o_r, w_mix, mix_post, pre, w_in, w_out, post):
    n, d = h.shape
    tile = pl.BlockSpec((TOKEN_TILE, d), lambda i: (i, 0))
    half = pl.BlockSpec((TOKEN_TILE, ATTN_WIDTH), lambda i: (i, 0))
    return pl.pallas_call(
        _mix_ffn_body,
        grid=(n // TOKEN_TILE,),
        in_specs=[tile, half, half, _resident(w_mix.shape), _resident(mix_post.shape),
                  _resident(pre.shape), _resident(w_in.shape), _resident(w_out.shape),
                  _resident(post.shape)],
        out_specs=tile,
        out_shape=jax.ShapeDtypeStruct((n, d), F32),
        scratch_shapes=[pltpu.VMEM((TOKEN_TILE, d), F32)],
        compiler_params=_params("parallel"),
        name="mix_ffn",
    )(h, o_a, o_r, w_mix, mix_post, pre, w_in, w_out, post)


PROJ_WIDTH = 512
MIXPROJ_DTYPES = [BF16] * 8 + [F32] * 2


def _silu(x):
    return x * jax.nn.sigmoid(x)


def _mixproj_body(h_ref, norm_ref, w_ref, lb_ref, qa_ref, ka_ref, va_ref, qr_ref, ir_ref, gr_ref,
                  kf_ref, kb_ref, gf_ref, gb_ref):
    u = _rms(h_ref[...], norm_ref[...]).astype(BF16)

    def proj(j):
        return _dot(u, w_ref[:, j * PROJ_WIDTH:(j + 1) * PROJ_WIDTH])

    qa_ref[...] = (proj(0) * (ATTN_HALF_DIM ** -0.5 * LOG2E)).astype(BF16)
    ka_ref[...] = proj(1).astype(BF16)
    va_ref[...] = proj(2).astype(BF16)
    qr_ref[...] = _silu(proj(3)).astype(BF16)
    ir_ref[...] = proj(4).astype(BF16)
    for direction, (k_ref, g_ref) in enumerate(((kf_ref, gf_ref), (kb_ref, gb_ref))):
        lb = lb_ref[direction]
        sig = jax.nn.sigmoid(proj(5 + direction))
        k_ref[...] = ((1.0 - lb) * (1.0 - sig)).astype(BF16)
        g_ref[...] = jnp.log(lb + (1.0 - lb) * sig)
    gr_ref[...] = _silu(proj(7)).astype(BF16)


def _mixproj(h, norm, w, lb):
    n, d = h.shape
    tile = pl.BlockSpec((TOKEN_TILE, d), lambda i: (i, 0))
    out_tile = pl.BlockSpec((TOKEN_TILE, PROJ_WIDTH), lambda i: (i, 0))
    return pl.pallas_call(
        _mixproj_body,
        grid=(n // TOKEN_TILE,),
        in_specs=[tile, _resident(norm.shape), _resident(w.shape), _resident(lb.shape)],
        out_specs=[out_tile] * len(MIXPROJ_DTYPES),
        out_shape=[jax.ShapeDtypeStruct((n, PROJ_WIDTH), dt) for dt in MIXPROJ_DTYPES],
        compiler_params=_params("parallel"),
        name="mixproj",
    )(h, norm, w, lb)


def _rel_bucket(rel):
    nb = N_BUCKETS // 2
    max_exact = nb // 2
    side = jnp.where(rel > 0, nb, 0)
    n = jnp.abs(rel)
    nf = jnp.maximum(n, 1).astype(jnp.float32)
    large = max_exact + (jnp.log(nf / max_exact) / math.log(MAX_DISTANCE / max_exact)
                         * (nb - max_exact)).astype(jnp.int32)
    large = jnp.minimum(large, nb - 1)
    return side + jnp.where(n < max_exact, n, large)


def _bias_tables(rel_bias, seq):
    nb = N_BUCKETS // 2
    n_q = seq // Q_TILE
    assert 2 * n_q <= HEAD_DIM and BIAS_HALO >= MAX_DISTANCE
    rb = rel_bias.astype(F32) * LOG2E
    i = jnp.arange(Q_TILE, dtype=jnp.int32)[:, None]
    j = jnp.arange(Q_TILE + 2 * BIAS_HALO, dtype=jnp.int32)[None, :]
    bucket = _rel_bucket(j - BIAS_HALO - i)
    band = jnp.zeros((ATTN_HEADS, Q_TILE, Q_TILE + 2 * BIAS_HALO), F32)
    for n in range(N_BUCKETS):
        band = jnp.where(bucket[None] == n, rb[n][:, None, None], band)
    before = rb[nb - 1][:, None, None]
    after = rb[2 * nb - 1][:, None, None]
    near = band - jnp.where(j[None] < BIAS_HALO, before, after)
    key_pos = jnp.arange(seq, dtype=jnp.int32)[None, :, None]
    tile_start = (jnp.arange(n_q, dtype=jnp.int32) * Q_TILE)[None, None, :]
    far = jnp.where(key_pos < tile_start, before, after)
    hi = far.astype(BF16)
    lo = (far - hi.astype(F32)).astype(BF16)
    pad = jnp.zeros((ATTN_HEADS, seq, HEAD_DIM - 2 * n_q), BF16)
    return jnp.concatenate([hi, lo, pad], axis=-1), near


def _attn_body(lam_ref, q_ref, k_ref, v_ref, far_ref, near_ref, norm_ref, o_ref,
               kx_ref, vx_ref, qx_ref, s_ref, p_ref):
    qi = pl.program_id(2)
    n_q = pl.num_programs(2)
    seq = k_ref.shape[1]

    @pl.when(qi == 0)
    def _():
        s_ref[:, :BIAS_HALO] = jnp.zeros((2 * Q_TILE, BIAS_HALO), F32)
        s_ref[:, BIAS_HALO + seq:] = jnp.zeros((2 * Q_TILE, BIAS_HALO), F32)
        kx_ref[:, :HEAD_DIM] = k_ref[0]
        kx_ref[:, HEAD_DIM:] = far_ref[0]
        vx_ref[:, :HEAD_DIM] = v_ref[0]
        vx_ref[:, HEAD_DIM:] = jnp.ones((seq, HEAD_DIM), BF16)

    q = q_ref[0]
    lane = lax.broadcasted_iota(jnp.int32, q.shape, 1)
    zero = jnp.zeros_like(q)
    tile_onehot = jnp.where((lane == qi) | (lane == n_q + qi), 1.0, 0.0).astype(BF16)
    qx_ref[:Q_TILE, :HEAD_DIM] = jnp.where(lane < ATTN_HALF_DIM, q, zero)
    qx_ref[Q_TILE:, :HEAD_DIM] = jnp.where(lane >= ATTN_HALF_DIM, q, zero)
    qx_ref[:Q_TILE, HEAD_DIM:] = tile_onehot
    qx_ref[Q_TILE:, HEAD_DIM:] = tile_onehot

    keys = slice(BIAS_HALO, BIAS_HALO + seq)
    window = pl.ds(pl.multiple_of(qi * Q_TILE, Q_TILE), Q_TILE + 2 * BIAS_HALO)
    for c in range(2):
        rows = slice(c * Q_TILE, (c + 1) * Q_TILE)
        s_ref[rows, keys] = lax.dot_general(qx_ref[rows, :], kx_ref[...], NT_DIMS,
                                            preferred_element_type=F32)
        s_ref[rows, window] += near_ref[0]

    outs = []
    for c in range(2):
        for rb in range(Q_TILE // SOFTMAX_ROWS):
            rows = slice(c * Q_TILE + rb * SOFTMAX_ROWS, c * Q_TILE + (rb + 1) * SOFTMAX_ROWS)
            m = jnp.max(s_ref[rows, keys], axis=-1, keepdims=True)
            p_ref[rows, :] = jnp.exp2((s_ref[rows, keys] - m).astype(BF16))
        rows = slice(c * Q_TILE, (c + 1) * Q_TILE)
        acc = _dot(p_ref[rows, :], vx_ref[...])
        outs.append(acc[:, :HEAD_DIM] / acc[:, HEAD_DIM:])
    o = outs[0] - lam_ref[0] * outs[1]
    o_ref[0] = (_rms(o, norm_ref[...]) * (1.0 - LAMBDA_INIT)).astype(o_ref.dtype)


def _attention(q, k, v, far, near, lam, head_norm):
    b, s, _ = q.shape
    smem = pl.BlockSpec(memory_space=pltpu.SMEM)
    return pl.pallas_call(
        _attn_body,
        grid=(b, ATTN_HEADS, s // Q_TILE),
        in_specs=[smem,
                  pl.BlockSpec((1, Q_TILE, HEAD_DIM), lambda bi, h, i: (bi, i, h)),
                  pl.BlockSpec((1, s, HEAD_DIM), lambda bi, h, i: (bi, 0, h)),
                  pl.BlockSpec((1, s, HEAD_DIM), lambda bi, h, i: (bi, 0, h)),
                  pl.BlockSpec((1, s, HEAD_DIM), lambda bi, h, i: (h, 0, 0)),
                  pl.BlockSpec((1, Q_TILE, Q_TILE + 2 * BIAS_HALO), lambda bi, h, i: (h, 0, 0)),
                  _resident(head_norm.shape)],
        out_specs=pl.BlockSpec((1, Q_TILE, HEAD_DIM), lambda bi, h, i: (bi, i, h)),
        out_shape=jax.ShapeDtypeStruct((b, s, ATTN_WIDTH), BF16),
        scratch_shapes=[pltpu.VMEM((s, 2 * HEAD_DIM), BF16), pltpu.VMEM((s, 2 * HEAD_DIM), BF16),
                        pltpu.VMEM((2 * Q_TILE, 2 * HEAD_DIM), BF16),
                        pltpu.VMEM((2 * Q_TILE, s + 2 * BIAS_HALO), F32),
                        pltpu.VMEM((2 * Q_TILE, s), BF16)],
        compiler_params=_params("parallel", "parallel", "arbitrary"),
        name="attn",
    )(lam, q, k, v, far, near, head_norm)


def _chunk_triangles():
    t = np.arange(RNN_GROUP)[:, None]
    s = np.arange(RNN_GROUP)[None, :]
    same = (t // CHUNK) == (s // CHUNK)
    return np.stack([same & (s <= t), same & (s >= t)]).astype(np.float32)


def _hgrn2_body(q_ref, v_ref, gate_ref, kf_ref, kb_ref, gf_ref, gb_ref, tri_ref, norm_ref, o_ref,
                acc_ref, qd_ref, u_ref, dl_ref, sp_ref):
    seq = q_ref.shape[1]
    n_groups = seq // RNN_GROUP
    n_chunks = seq // CHUNK
    per_group = RNN_GROUP // CHUNK
    k_refs = (kf_ref, kb_ref)
    g_refs = (gf_ref, gb_ref)
    mid_row = (CHUNK // 2 - 1, CHUNK // 2)
    last_row = (CHUNK - 1, 0)

    def chunked(t):
        return t.reshape(per_group, CHUNK, HEAD_DIM)

    def flat(t):
        return t.reshape(RNN_GROUP, HEAD_DIM)

    def group_rows(gi):
        return pl.ds(pl.multiple_of(gi * RNN_GROUP, RNN_GROUP), RNN_GROUP)

    def chunk_local(gi, carry):
        rows = group_rows(gi)
        q = chunked(q_ref[0, rows, :].astype(F32))
        v = v_ref[0, rows, :]
        intra = None
        for d in range(2):
            log_f = g_refs[d][0, rows, :]
            hi = log_f.astype(BF16)
            lo = (log_f - hi.astype(F32)).astype(BF16)
            both = _dot(tri_ref[d], jnp.concatenate([hi, lo], axis=1))
            cum = chunked(both[:, :HEAD_DIM] + both[:, HEAD_DIM:])
            cum_mid = cum[:, mid_row[d]:mid_row[d] + 1, :]
            cum_last = cum[:, last_row[d]:last_row[d] + 1, :]
            k = chunked(k_refs[d][0, rows, :].astype(F32))
            q_mid = q * jnp.exp(cum - cum_mid)
            k_mid = k * jnp.exp(cum_mid - cum)
            a = lax.dot_general(flat(q_mid).astype(BF16), flat(k_mid).astype(BF16), NT_DIMS,
                                preferred_element_type=F32).astype(BF16)
            a = jnp.where(tri_ref[d] > 0, a, jnp.zeros_like(a))
            o = _dot(a, v)
            intra = o if intra is None else intra + o
            qd_ref[d, rows, :] = flat(q_mid * jnp.exp(cum_mid)).astype(BF16)
            k_dec = (k_mid * jnp.exp(cum_last - cum_mid)).astype(BF16)
            decay = jnp.exp(cum_last)
            for j in range(per_group):
                ci = gi * per_group + j
                u_ref[d, ci] = lax.dot_general(v[j * CHUNK:(j + 1) * CHUNK], k_dec[j], TN_DIMS,
                                               preferred_element_type=F32)
                dl_ref[d, ci] = decay[j]
        acc_ref[rows, :] = intra
        return carry
    lax.fori_loop(0, n_groups, chunk_local, 0, unroll=2)

    def scan(i, states):
        new = []
        for d, state in enumerate(states):
            ci = i if d == 0 else n_chunks - 1 - i
            sp_ref[d, ci] = state.astype(BF16)
            new.append(state * dl_ref[d, ci] + u_ref[d, ci])
        return tuple(new)
    zero = jnp.zeros((HEAD_DIM, HEAD_DIM), F32)
    lax.fori_loop(0, n_chunks, scan, (zero, zero))

    def outputs(gi, carry):
        rows = group_rows(gi)
        total = acc_ref[rows, :]
        for d in range(2):
            entering = [lax.dot_general(qd_ref[d, pl.ds(pl.multiple_of(gi * RNN_GROUP + j * CHUNK, CHUNK), CHUNK), :],
                                        sp_ref[d, gi * per_group + j], NT_DIMS,
                                        preferred_element_type=F32) for j in range(per_group)]
            total = total + jnp.concatenate(entering, axis=0)
        gate = gate_ref[0, rows, :].astype(F32)
        o_ref[0, rows, :] = (_rms(total, norm_ref[...]) * gate).astype(o_ref.dtype)
        return carry
    lax.fori_loop(0, n_groups, outputs, 0, unroll=4)


def _hgrn2(q, v, gate, k_fw, k_bw, g_fw, g_bw, head_norm):
    b, s, _ = q.shape
    n_chunks = s // CHUNK
    tri = jnp.asarray(_chunk_triangles(), BF16)
    seq_block = pl.BlockSpec((1, s, HEAD_DIM), lambda bi, h: (bi, 0, h))
    return pl.pallas_call(
        _hgrn2_body,
        grid=(b, RNN_HEADS),
        in_specs=[seq_block] * 7 + [_resident(tri.shape), _resident(head_norm.shape)],
        out_specs=seq_block,
        out_shape=jax.ShapeDtypeStruct((b, s, RNN_WIDTH), BF16),
        scratch_shapes=[pltpu.VMEM((s, HEAD_DIM), F32),
                        pltpu.VMEM((2, s, HEAD_DIM), BF16),
                        pltpu.VMEM((2, n_chunks, HEAD_DIM, HEAD_DIM), F32),
                        pltpu.VMEM((2, n_chunks, 1, HEAD_DIM), F32),
                        pltpu.VMEM((2, n_chunks, HEAD_DIM, HEAD_DIM), BF16)],
        compiler_params=_params("parallel", "parallel"),
        name="hgrn2",
    )(q, v, gate, k_fw, k_bw, g_fw, g_bw, tri, head_norm)


def kernel(x, rel_bias, lb_logits, ffn1_pre_norm, ffn1_w_in, ffn1_w_out, ffn1_post_norm, mix_pre_norm, w_mix_in, lambda_q1, lambda_k1, lambda_q2, lambda_k2, attn_head_norm, rnn_head_norm, w_mix_out, mix_post_norm, ffn2_pre_norm, ffn2_w_in, ffn2_w_out, ffn2_post_norm):
    b, s, d = x.shape
    layer = 0
    row = lambda t: t[layer].reshape(1, -1).astype(F32)

    h1 = _ffn(x.reshape(b * s, d), row(ffn1_pre_norm), ffn1_w_in[layer].astype(BF16),
              ffn1_w_out[layer].astype(BF16), row(ffn1_post_norm))

    lb = jnp.cumsum(jax.nn.softmax(lb_logits.astype(F32), axis=1), axis=1)[:, layer]
    q_a, k_a, v_a, q_r, i_r, g_r, k_fw, k_bw, g_fw, g_bw = _mixproj(
        h1, row(mix_pre_norm), w_mix_in[layer].astype(BF16), lb.reshape(2, 1, RNN_WIDTH))
    seq = lambda t: t.reshape(b, s, t.shape[-1])

    lam = (jnp.exp(jnp.sum(lambda_q1[layer] * lambda_k1[layer]).astype(F32))
           - jnp.exp(jnp.sum(lambda_q2[layer] * lambda_k2[layer]).astype(F32))
           + LAMBDA_INIT).reshape(1)
    far, near = _bias_tables(rel_bias, s)
    o_a = _attention(seq(q_a), seq(k_a), seq(v_a), far, near, lam, row(attn_head_norm))

    o_r = _hgrn2(seq(q_r), seq(i_r), seq(g_r), seq(k_fw), seq(k_bw), seq(g_fw), seq(g_bw),
                 row(rnn_head_norm))

    y = _mix_ffn(h1, o_a.reshape(b * s, ATTN_WIDTH), o_r.reshape(b * s, RNN_WIDTH),
                 w_mix_out[layer].astype(BF16), row(mix_post_norm), row(ffn2_pre_norm),
                 ffn2_w_in[layer].astype(BF16), ffn2_w_out[layer].astype(BF16),
                 row(ffn2_post_norm))
    return y.reshape(b, s, d).astype(x.dtype)
```

```python
import functools
import math

import numpy as np
import jax
import jax.numpy as jnp
from jax import lax
from jax.experimental import pallas as pl
from jax.experimental.pallas import tpu as pltpu

F32 = jnp.float32
BF16 = jnp.bfloat16

D_MODEL = 1024
ATTN_WIDTH = 512
RNN_WIDTH = 512
ATTN_HEADS = 4
ATTN_HALF_DIM = 64
HEAD_DIM = 128
RNN_HEADS = 4
D_FF = 2816
N_BUCKETS = 32
MAX_DISTANCE = 128
CHUNK = 64
EPS = 1e-6
LAMBDA_INIT = 0.8 - 0.6 * math.exp(-0.3 * 0)

VMEM_LIMIT_BYTES = 56 * 1024 * 1024
LANES = 128

TOKEN_TILE = 512
FF_CHUNK = 256
Q_TILE = 256
Q_TILES_PER_STEP = 2
KEY_CHUNK = 256
CHUNKS_PER_ITER = 8
NEAR_REACH = 1
LOG2E = math.log2(math.e)
RNN_GROUP = 256

NT_DIMS = (((1,), (1,)), ((), ()))
TN_DIMS = (((0,), (0,)), ((), ()))


def _rms(x, w):
    return x * lax.rsqrt(jnp.mean(x * x, axis=-1, keepdims=True) + EPS) * w


def _dot(a, b):
    return jnp.dot(a, b, preferred_element_type=F32)


def _resident(shape):
    return pl.BlockSpec(shape, lambda *_: (0,) * len(shape), pipeline_mode=pl.Buffered(1))


def _params(*semantics):
    return pltpu.CompilerParams(dimension_semantics=semantics, vmem_limit_bytes=VMEM_LIMIT_BYTES)


def _swiglu_half_step(h, pre_ref, w_in_ref, w_out_ref, post_ref, acc_ref):
    u = _rms(h, pre_ref[...]).astype(BF16)
    for c in range(D_FF // FF_CHUNK):
        lo = c * FF_CHUNK
        gate = _dot(u, w_in_ref[:, lo:lo + FF_CHUNK])
        up = _dot(u, w_in_ref[:, D_FF + lo:D_FF + lo + FF_CHUNK])
        act = (gate * jax.nn.sigmoid(gate) * up).astype(BF16)
        part = _dot(act, w_out_ref[lo:lo + FF_CHUNK, :])
        if c == 0:
            acc_ref[...] = part
        else:
            acc_ref[...] += part
    return h + 0.5 * _rms(acc_ref[...], post_ref[...])


def _ffn_body(x_ref, pre_ref, w_in_ref, w_out_ref, post_ref, o_ref, acc_ref):
    o_ref[...] = _swiglu_half_step(x_ref[...], pre_ref, w_in_ref, w_out_ref, post_ref, acc_ref)


def _ffn(x, pre, w_in, w_out, post):
    n, d = x.shape
    tile = pl.BlockSpec((TOKEN_TILE, d), lambda i: (i, 0))
    return pl.pallas_call(
        _ffn_body,
        grid=(n // TOKEN_TILE,),
        in_specs=[tile, _resident(pre.shape), _resident(w_in.shape), _resident(w_out.shape),
                  _resident(post.shape)],
        out_specs=tile,
        out_shape=jax.ShapeDtypeStruct((n, d), F32),
        scratch_shapes=[pltpu.VMEM((TOKEN_TILE, d), F32)],
        compiler_params=_params("parallel"),
        name="ffn",
    )(x, pre, w_in, w_out, post)


def _mix_ffn_body(h_ref, oa_ref, or_ref, w_mix_ref, mix_post_ref, pre_ref, w_in_ref, w_out_ref,
                  post_ref, o_ref, acc_ref):
    mixed = (_dot(oa_ref[...], w_mix_ref[:ATTN_WIDTH, :])
             + _dot(or_ref[...], w_mix_ref[ATTN_WIDTH:, :]))
    h = h_ref[...] + _rms(mixed, mix_post_ref[...])
    o_ref[...] = _swiglu_half_step(h, pre_ref, w_in_ref, w_out_ref, post_ref, acc_ref)


def _mix_ffn(h, o_a, o_r, w_mix, mix_post, pre, w_in, w_out, post):
    n, d = h.shape
    tile = pl.BlockSpec((TOKEN_TILE, d), lambda i: (i, 0))
    half = pl.BlockSpec((TOKEN_TILE, ATTN_WIDTH), lambda i: (i, 0))
    return pl.pallas_call(
        _mix_ffn_body,
        grid=(n // TOKEN_TILE,),
        in_specs=[tile, half, half, _resident(w_mix.shape), _resident(mix_post.shape),
                  _resident(pre.shape), _resident(w_in.shape), _resident(w_out.shape),
                  _resident(post.shape)],
        out_specs=tile,
        out_shape=jax.ShapeDtypeStruct((n, d), F32),
        scratch_shapes=[pltpu.VMEM((TOKEN_TILE, d), F32)],
        compiler_params=_params("parallel"),
        name="mix_ffn",
    )(h, o_a, o_r, w_mix, mix_post, pre, w_in, w_out, post)


PROJ_WIDTH = 512
MIXPROJ_DTYPES = [BF16] * 8 + [F32] * 2


def _silu(x):
    return x * jax.nn.sigmoid(x)


def _mixproj_body(h_ref, norm_ref, w_ref, lb_ref, qa_ref, ka_ref, va_ref, qr_ref, ir_ref, gr_ref,
                  kf_ref, kb_ref, gf_ref, gb_ref):
    u = _rms(h_ref[...], norm_ref[...]).astype(BF16)

    def proj(j):
        return _dot(u, w_ref[:, j * PROJ_WIDTH:(j + 1) * PROJ_WIDTH])

    qa_ref[...] = (proj(0) * (ATTN_HALF_DIM ** -0.5 * LOG2E)).astype(BF16)
    ka_ref[...] = proj(1).astype(BF16)
    va_ref[...] = proj(2).astype(BF16)
    qr_ref[...] = _silu(proj(3)).astype(BF16)
    ir_ref[...] = proj(4).astype(BF16)
    for direction, (k_ref, g_ref) in enumerate(((kf_ref, gf_ref), (kb_ref, gb_ref))):
        lb = lb_ref[direction]
        sig = jax.nn.sigmoid(proj(5 + direction))
        k_ref[...] = ((1.0 - lb) * (1.0 - sig)).astype(BF16)
        g_ref[...] = jnp.log(lb + (1.0 - lb) * sig)
    gr_ref[...] = _silu(proj(7)).astype(BF16)


def _mixproj(h, norm, w, lb):
    n, d = h.shape
    tile = pl.BlockSpec((TOKEN_TILE, d), lambda i: (i, 0))
    out_tile = pl.BlockSpec((TOKEN_TILE, PROJ_WIDTH), lambda i: (i, 0))
    return pl.pallas_call(
        _mixproj_body,
        grid=(n // TOKEN_TILE,),
        in_specs=[tile, _resident(norm.shape), _resident(w.shape), _resident(lb.shape)],
        out_specs=[out_tile] * len(MIXPROJ_DTYPES),
        out_shape=[jax.ShapeDtypeStruct((n, PROJ_WIDTH), dt) for dt in MIXPROJ_DTYPES],
        compiler_params=_params("parallel"),
        name="mixproj",
    )(h, norm, w, lb)


def _rel_bucket(rel):
    nb = N_BUCKETS // 2
    max_exact = nb // 2
    side = jnp.where(rel > 0, nb, 0)
    n = jnp.abs(rel)
    nf = jnp.maximum(n, 1).astype(jnp.float32)
    large = max_exact + (jnp.log(nf / max_exact) / math.log(MAX_DISTANCE / max_exact)
                         * (nb - max_exact)).astype(jnp.int32)
    large = jnp.minimum(large, nb - 1)
    return side + jnp.where(n < max_exact, n, large)


def _bias_tables(rel_bias, seq):
    nb = N_BUCKETS // 2
    n_q = seq // Q_TILE
    assert 2 * n_q <= HEAD_DIM and KEY_CHUNK == Q_TILE and NEAR_REACH * KEY_CHUNK >= MAX_DISTANCE
    rb = rel_bias.astype(F32) * LOG2E
    width = (2 * NEAR_REACH + 3) * KEY_CHUNK
    own = (NEAR_REACH + 1) * KEY_CHUNK
    i = jnp.arange(Q_TILE, dtype=jnp.int32)[:, None]
    j = jnp.arange(width, dtype=jnp.int32)[None, :]
    bucket = _rel_bucket(j - own - i)
    band = jnp.zeros((ATTN_HEADS, Q_TILE, width), F32)
    for n in range(N_BUCKETS):
        band = jnp.where(bucket[None] == n, rb[n][:, None, None], band)
    before = rb[nb - 1][:, None, None]
    after = rb[2 * nb - 1][:, None, None]
    near = band - jnp.where(j[None] < own, before, after)
    key_pos = jnp.arange(seq, dtype=jnp.int32)[None, :, None]
    tile_start = (jnp.arange(n_q, dtype=jnp.int32) * Q_TILE)[None, None, :]
    far = jnp.where(key_pos < tile_start, before, after)
    hi = far.astype(BF16)
    lo = (far - hi.astype(F32)).astype(BF16)
    pad = jnp.zeros((ATTN_HEADS, seq, HEAD_DIM - 2 * n_q), BF16)
    return jnp.concatenate([hi, lo, pad], axis=-1), near


N_ATTN_UNITS = 2 * Q_TILES_PER_STEP


def _attn_body(lam_ref, q_ref, k_ref, v_ref, far_ref, near_ref, norm_ref, o_ref,
               kx_ref, vx_ref, qx_ref, m_ref, *unit_refs):
    step = pl.program_id(2)
    seq = k_ref.shape[1]
    n_q = seq // Q_TILE
    n_kc = seq // KEY_CHUNK
    s_refs = unit_refs[:N_ATTN_UNITS]
    p_refs = unit_refs[N_ATTN_UNITS:]

    @pl.when(step == 0)
    def _():
        kx_ref[:, :HEAD_DIM] = k_ref[0]
        kx_ref[:, HEAD_DIM:] = far_ref[0]
        vx_ref[:, :HEAD_DIM] = v_ref[0]
        vx_ref[:, HEAD_DIM:] = jnp.ones((seq, HEAD_DIM), BF16)

    lane = lax.broadcasted_iota(jnp.int32, (Q_TILE, HEAD_DIM), 1)
    zero = jnp.zeros((Q_TILE, HEAD_DIM), BF16)
    for t in range(Q_TILES_PER_STEP):
        q = q_ref[0, t * Q_TILE:(t + 1) * Q_TILE, :]
        tile = step * Q_TILES_PER_STEP + t
        tile_onehot = jnp.where((lane == tile) | (lane == n_q + tile), 1.0, 0.0).astype(BF16)
        qx_ref[2 * t, :, :HEAD_DIM] = jnp.where(lane < ATTN_HALF_DIM, q, zero)
        qx_ref[2 * t + 1, :, :HEAD_DIM] = jnp.where(lane >= ATTN_HALF_DIM, q, zero)
        qx_ref[2 * t, :, HEAD_DIM:] = tile_onehot
        qx_ref[2 * t + 1, :, HEAD_DIM:] = tile_onehot
    m_ref[...] = jnp.full(m_ref.shape, -jnp.inf, F32)

    chunks_per_iter = min(CHUNKS_PER_ITER, n_kc)
    assert n_kc % chunks_per_iter == 0

    def score_chunks(it, carry):
        for j in range(chunks_per_iter):
            kc = it * chunks_per_iter + j
            cols = pl.ds(pl.multiple_of(kc * KEY_CHUNK, KEY_CHUNK), KEY_CHUNK)
            for t in range(Q_TILES_PER_STEP):
                rel = jnp.clip(kc - (step * Q_TILES_PER_STEP + t), -NEAR_REACH - 1, NEAR_REACH + 1)
                near = near_ref[0, :, pl.ds(pl.multiple_of((rel + NEAR_REACH + 1) * KEY_CHUNK, KEY_CHUNK),
                                            KEY_CHUNK)]
                for u in (2 * t, 2 * t + 1):
                    s = lax.dot_general(qx_ref[u], kx_ref[cols, :], NT_DIMS,
                                        preferred_element_type=F32) + near
                    s_refs[u][:, cols] = s
                    m_ref[u] = jnp.maximum(m_ref[u], jnp.maximum(s[:, :LANES], s[:, LANES:]))
        return carry
    lax.fori_loop(0, n_kc // chunks_per_iter, score_chunks, 0)

    outs = []
    for u in range(N_ATTN_UNITS):
        m = jnp.max(m_ref[u], axis=-1, keepdims=True)
        for kc in range(n_kc):
            cols = slice(kc * KEY_CHUNK, (kc + 1) * KEY_CHUNK)
            p_refs[u][:, cols] = jnp.exp2((s_refs[u][:, cols] - m).astype(BF16))
        acc = _dot(p_refs[u][...], vx_ref[...])
        outs.append(acc[:, :HEAD_DIM] / acc[:, HEAD_DIM:])
    for t in range(Q_TILES_PER_STEP):
        o = outs[2 * t] - lam_ref[0] * outs[2 * t + 1]
        o_ref[0, t * Q_TILE:(t + 1) * Q_TILE, :] = (
            _rms(o, norm_ref[...]) * (1.0 - LAMBDA_INIT)).astype(o_ref.dtype)


def _attention(q, k, v, far, near, lam, head_norm):
    b, s, _ = q.shape
    rows = Q_TILES_PER_STEP * Q_TILE
    smem = pl.BlockSpec(memory_space=pltpu.SMEM)
    return pl.pallas_call(
        _attn_body,
        grid=(b, ATTN_HEADS, s // rows),
        in_specs=[smem,
                  pl.BlockSpec((1, rows, HEAD_DIM), lambda bi, h, i: (bi, i, h)),
                  pl.BlockSpec((1, s, HEAD_DIM), lambda bi, h, i: (bi, 0, h)),
                  pl.BlockSpec((1, s, HEAD_DIM), lambda bi, h, i: (bi, 0, h)),
                  pl.BlockSpec((1, s, HEAD_DIM), lambda bi, h, i: (h, 0, 0)),
                  pl.BlockSpec((1,) + near.shape[1:], lambda bi, h, i: (h, 0, 0)),
                  _resident(head_norm.shape)],
        out_specs=pl.BlockSpec((1, rows, HEAD_DIM), lambda bi, h, i: (bi, i, h)),
        out_shape=jax.ShapeDtypeStruct((b, s, ATTN_WIDTH), BF16),
        scratch_shapes=[pltpu.VMEM((s, 2 * HEAD_DIM), BF16), pltpu.VMEM((s, 2 * HEAD_DIM), BF16),
                        pltpu.VMEM((N_ATTN_UNITS, Q_TILE, 2 * HEAD_DIM), BF16),
                        pltpu.VMEM((N_ATTN_UNITS, Q_TILE, LANES), F32)]
                       + [pltpu.VMEM((Q_TILE, s), F32)] * N_ATTN_UNITS
                       + [pltpu.VMEM((Q_TILE, s), BF16)] * N_ATTN_UNITS,
        compiler_params=_params("parallel", "parallel", "arbitrary"),
        name="attn",
    )(lam, q, k, v, far, near, head_norm)


def _chunk_triangles():
    t = np.arange(RNN_GROUP)[:, None]
    s = np.arange(RNN_GROUP)[None, :]
    same = (t // CHUNK) == (s // CHUNK)
    return np.stack([same & (s <= t), same & (s >= t)]).astype(np.float32)


def _hgrn2_body(q_ref, v_ref, gate_ref, kf_ref, kb_ref, gf_ref, gb_ref, tri_ref, norm_ref, o_ref,
                acc_ref, qd_ref, u_ref, dl_ref, sp_ref):
    seq = q_ref.shape[1]
    n_groups = seq // RNN_GROUP
    n_chunks = seq // CHUNK
    per_group = RNN_GROUP // CHUNK
    k_refs = (kf_ref, kb_ref)
    g_refs = (gf_ref, gb_ref)
    mid_row = (CHUNK // 2 - 1, CHUNK // 2)
    last_row = (CHUNK - 1, 0)

    def chunked(t):
        return t.reshape(per_group, CHUNK, HEAD_DIM)

    def flat(t):
        return t.reshape(RNN_GROUP, HEAD_DIM)

    def group_rows(gi):
        return pl.ds(pl.multiple_of(gi * RNN_GROUP, RNN_GROUP), RNN_GROUP)

    def chunk_local(gi, carry):
        rows = group_rows(gi)
        q = chunked(q_ref[0, rows, :].astype(F32))
        v = v_ref[0, rows, :]
        intra = None
        for d in range(2):
            log_f = g_refs[d][0, rows, :]
            hi = log_f.astype(BF16)
            lo = (log_f - hi.astype(F32)).astype(BF16)
            both = _dot(tri_ref[d], jnp.concatenate([hi, lo], axis=1))
            cum = chunked(both[:, :HEAD_DIM] + both[:, HEAD_DIM:])
            cum_mid = cum[:, mid_row[d]:mid_row[d] + 1, :]
            cum_last = cum[:, last_row[d]:last_row[d] + 1, :]
            k = chunked(k_refs[d][0, rows, :].astype(F32))
            q_mid = q * jnp.exp(cum - cum_mid)
            k_mid = k * jnp.exp(cum_mid - cum)
            a = lax.dot_general(flat(q_mid).astype(BF16), flat(k_mid).astype(BF16), NT_DIMS,
                                preferred_element_type=F32).astype(BF16)
            a = jnp.where(tri_ref[d] > 0, a, jnp.zeros_like(a))
            o = _dot(a, v)
            intra = o if intra is None else intra + o
            qd_ref[d, rows, :] = flat(q_mid * jnp.exp(cum_mid)).astype(BF16)
            k_dec = (k_mid * jnp.exp(cum_last - cum_mid)).astype(BF16)
            decay = jnp.exp(cum_last)
            for j in range(per_group):
                ci = gi * per_group + j
                u_ref[d, ci] = lax.dot_general(v[j * CHUNK:(j + 1) * CHUNK], k_dec[j], TN_DIMS,
                                               preferred_element_type=F32)
                dl_ref[d, ci] = decay[j]
        acc_ref[rows, :] = intra
        return carry
    lax.fori_loop(0, n_groups, chunk_local, 0, unroll=2)

    def scan(i, states):
        new = []
        for d, state in enumerate(states):
            ci = i if d == 0 else n_chunks - 1 - i
            sp_ref[d, ci] = state.astype(BF16)
            new.append(state * dl_ref[d, ci] + u_ref[d, ci])
        return tuple(new)
    zero = jnp.zeros((HEAD_DIM, HEAD_DIM), F32)
    lax.fori_loop(0, n_chunks, scan, (zero, zero))

    def outputs(gi, carry):
        rows = group_rows(gi)
        total = acc_ref[rows, :]
        for d in range(2):
            entering = [lax.dot_general(qd_ref[d, pl.ds(pl.multiple_of(gi * RNN_GROUP + j * CHUNK, CHUNK), CHUNK), :],
                                        sp_ref[d, gi * per_group + j], NT_DIMS,
                                        preferred_element_type=F32) for j in range(per_group)]
            total = total + jnp.concatenate(entering, axis=0)
        gate = gate_ref[0, rows, :].astype(F32)
        o_ref[0, rows, :] = (_rms(total, norm_ref[...]) * gate).astype(o_ref.dtype)
        return carry
    lax.fori_loop(0, n_groups, outputs, 0, unroll=4)


def _hgrn2(q, v, gate, k_fw, k_bw, g_fw, g_bw, head_norm):
    b, s, _ = q.shape
    n_chunks = s // CHUNK
    tri = jnp.asarray(_chunk_triangles(), BF16)
    seq_block = pl.BlockSpec((1, s, HEAD_DIM), lambda bi, h: (bi, 0, h))
    return pl.pallas_call(
        _hgrn2_body,
        grid=(b, RNN_HEADS),
        in_specs=[seq_block] * 7 + [_resident(tri.shape), _resident(head_norm.shape)],
        out_specs=seq_block,
        out_shape=jax.ShapeDtypeStruct((b, s, RNN_WIDTH), BF16),
        scratch_shapes=[pltpu.VMEM((s, HEAD_DIM), F32),
                        pltpu.VMEM((2, s, HEAD_DIM), BF16),
                        pltpu.VMEM((2, n_chunks, HEAD_DIM, HEAD_DIM), F32),
                        pltpu.VMEM((2, n_chunks, 1, HEAD_DIM), F32),
                        pltpu.VMEM((2, n_chunks, HEAD_DIM, HEAD_DIM), BF16)],
        compiler_params=_params("parallel", "parallel"),
        name="hgrn2",
    )(q, v, gate, k_fw, k_bw, g_fw, g_bw, tri, head_norm)


def kernel(x, rel_bias, lb_logits, ffn1_pre_norm, ffn1_w_in, ffn1_w_out, ffn1_post_norm, mix_pre_norm, w_mix_in, lambda_q1, lambda_k1, lambda_q2, lambda_k2, attn_head_norm, rnn_head_norm, w_mix_out, mix_post_norm, ffn2_pre_norm, ffn2_w_in, ffn2_w_out, ffn2_post_norm):
    b, s, d = x.shape
    layer = 0
    row = lambda t: t[layer].reshape(1, -1).astype(F32)

    h1 = _ffn(x.reshape(b * s, d), row(ffn1_pre_norm), ffn1_w_in[layer].astype(BF16),
              ffn1_w_out[layer].astype(BF16), row(ffn1_post_norm))

    lb = jnp.cumsum(jax.nn.softmax(lb_logits.astype(F32), axis=1), axis=1)[:, layer]
    q_a, k_a, v_a, q_r, i_r, g_r, k_fw, k_bw, g_fw, g_bw = _mixproj(
        h1, row(mix_pre_norm), w_mix_in[layer].astype(BF16), lb.reshape(2, 1, RNN_WIDTH))
    seq = lambda t: t.reshape(b, s, t.shape[-1])

    lam = (jnp.exp(jnp.sum(lambda_q1[layer] * lambda_k1[layer]).astype(F32))
           - jnp.exp(jnp.sum(lambda_q2[layer] * lambda_k2[layer]).astype(F32))
           + LAMBDA_INIT).reshape(1)
    far, near = _bias_tables(rel_bias, s)
    o_a = _attention(seq(q_a), seq(k_a), seq(v_a), far, near, lam, row(attn_head_norm))

    o_r = _hgrn2(seq(q_r), seq(i_r), seq(g_r), seq(k_fw), seq(k_bw), seq(g_fw), seq(g_bw),
                 row(rnn_head_norm))

    y = _mix_ffn(h1, o_a.reshape(b * s, ATTN_WIDTH), o_r.reshape(b * s, RNN_WIDTH),
                 w_mix_out[layer].astype(BF16), row(mix_post_norm), row(ffn2_pre_norm),
                 ffn2_w_in[layer].astype(BF16), ffn2_w_out[layer].astype(BF16),
                 row(ffn2_post_norm))
    return y.reshape(b, s, d).astype(x.dtype)
```

```python
import functools
import math

import numpy as np
import jax
import jax.numpy as jnp
from jax import lax
from jax.experimental import pallas as pl
from jax.experimental.pallas import tpu as pltpu

F32 = jnp.float32
BF16 = jnp.bfloat16

D_MODEL = 1024
ATTN_WIDTH = 512
RNN_WIDTH = 512
ATTN_HEADS = 4
ATTN_HALF_DIM = 64
HEAD_DIM = 128
RNN_HEADS = 4
D_FF = 2816
N_BUCKETS = 32
MAX_DISTANCE = 128
CHUNK = 64
EPS = 1e-6
LAMBDA_INIT = 0.8 - 0.6 * math.exp(-0.3 * 0)

VMEM_LIMIT_BYTES = 56 * 1024 * 1024
LANES = 128

TOKEN_TILE = 512
FF_CHUNK = 256
Q_TILE = 256
Q_TILES_PER_STEP = 2
KEY_CHUNK = 256
CHUNKS_PER_ITER = 8
NEAR_REACH = 1
LOG2E = math.log2(math.e)
RNN_GROUP = 256

NT_DIMS = (((1,), (1,)), ((), ()))
TN_DIMS = (((0,), (0,)), ((), ()))


def _rms(x, w):
    return x * lax.rsqrt(jnp.mean(x * x, axis=-1, keepdims=True) + EPS) * w


def _dot(a, b):
    return jnp.dot(a, b, preferred_element_type=F32)


def _resident(shape):
    return pl.BlockSpec(shape, lambda *_: (0,) * len(shape), pipeline_mode=pl.Buffered(1))


def _params(*semantics):
    return pltpu.CompilerParams(dimension_semantics=semantics, vmem_limit_bytes=VMEM_LIMIT_BYTES)


def _swiglu_half_step(h, pre_ref, w_in_ref, w_out_ref, post_ref, acc_ref):
    u = _rms(h, pre_ref[...]).astype(BF16)
    for c in range(D_FF // FF_CHUNK):
        lo = c * FF_CHUNK
        gate = _dot(u, w_in_ref[:, lo:lo + FF_CHUNK])
        up = _dot(u, w_in_ref[:, D_FF + lo:D_FF + lo + FF_CHUNK])
        act = (gate * jax.nn.sigmoid(gate) * up).astype(BF16)
        part = _dot(act, w_out_ref[lo:lo + FF_CHUNK, :])
        if c == 0:
            acc_ref[...] = part
        else:
            acc_ref[...] += part
    return h + 0.5 * _rms(acc_ref[...], post_ref[...])


def _ffn_body(x_ref, pre_ref, w_in_ref, w_out_ref, post_ref, o_ref, acc_ref):
    o_ref[...] = _swiglu_half_step(x_ref[...], pre_ref, w_in_ref, w_out_ref, post_ref, acc_ref)


def _ffn(x, pre, w_in, w_out, post):
    n, d = x.shape
    tile = pl.BlockSpec((TOKEN_TILE, d), lambda i: (i, 0))
    return pl.pallas_call(
        _ffn_body,
        grid=(n // TOKEN_TILE,),
        in_specs=[tile, _resident(pre.shape), _resident(w_in.shape), _resident(w_out.shape),
                  _resident(post.shape)],
        out_specs=tile,
        out_shape=jax.ShapeDtypeStruct((n, d), F32),
        scratch_shapes=[pltpu.VMEM((TOKEN_TILE, d), F32)],
        compiler_params=_params("parallel"),
        name="ffn",
    )(x, pre, w_in, w_out, post)


def _mix_ffn_body(h_ref, oa_ref, or_ref, w_mix_ref, mix_post_ref, pre_ref, w_in_ref, w_out_ref,
                  post_ref, o_ref, acc_ref):
    mixed = (_dot(oa_ref[...], w_mix_ref[:ATTN_WIDTH, :])
             + _dot(or_ref[...], w_mix_ref[ATTN_WIDTH:, :]))
    h = h_ref[...] + _rms(mixed, mix_post_ref[...])
    o_ref[...] = _swiglu_half_step(h, pre_ref, w_in_ref, w_out_ref, post_ref, acc_ref)


def _mix_ffn(h, o_a, o_r, w_mix, mix_post, pre, w_in, w_out, post):
    n, d = h.shape
    tile = pl.BlockSpec((TOKEN_TILE, d), lambda i: (i, 0))
    half = pl.BlockSpec((TOKEN_TILE, ATTN_WIDTH), lambda i: (i, 0))
    return pl.pallas_call(
        _mix_ffn_body,
        grid=(n // TOKEN_TILE,),
        in_specs=[tile, half, half, _resident(w_mix.shape), _resident(mix_post.shape),
                  _resident(pre.shape), _resident(w_in.shape), _resident(w_out.shape),
                  _resident(post.shape)],
        out_specs=tile,
        out_shape=jax.ShapeDtypeStruct((n, d), F32),
        scratch_shapes=[pltpu.VMEM((TOKEN_TILE, d), F32)],
        compiler_params=_params("parallel"),
        name="mix_ffn",
    )(h, o_a, o_r, w_mix, mix_post, pre, w_in, w_out, post)


PROJ_WIDTH = 512
MIXPROJ_DTYPES = [BF16] * 8 + [F32] * 2


def _sigmoid(x):
    return 0.5 * jnp.tanh(0.5 * x) + 0.5


def _silu(x):
    return x * _sigmoid(x)


def _mixproj_body(h_ref, norm_ref, w_ref, lb_ref, qa_ref, ka_ref, va_ref, qr_ref, ir_ref, gr_ref,
                  kf_ref, kb_ref, gf_ref, gb_ref):
    u = _rms(h_ref[...], norm_ref[...]).astype(BF16)

    def proj(j):
        return _dot(u, w_ref[:, j * PROJ_WIDTH:(j + 1) * PROJ_WIDTH])

    qa_ref[...] = (proj(0) * (ATTN_HALF_DIM ** -0.5 * LOG2E)).astype(BF16)
    ka_ref[...] = proj(1).astype(BF16)
    va_ref[...] = proj(2).astype(BF16)
    qr_ref[...] = _silu(proj(3)).astype(BF16)
    ir_ref[...] = proj(4).astype(BF16)
    for direction, (k_ref, g_ref) in enumerate(((kf_ref, gf_ref), (kb_ref, gb_ref))):
        lb = lb_ref[direction]
        sig = _sigmoid(proj(5 + direction))
        k_ref[...] = ((1.0 - lb) * (1.0 - sig)).astype(BF16)
        g_ref[...] = jnp.log(lb + (1.0 - lb) * sig)
    gr_ref[...] = _silu(proj(7)).astype(BF16)


def _mixproj(h, norm, w, lb):
    n, d = h.shape
    tile = pl.BlockSpec((TOKEN_TILE, d), lambda i: (i, 0))
    out_tile = pl.BlockSpec((TOKEN_TILE, PROJ_WIDTH), lambda i: (i, 0))
    return pl.pallas_call(
        _mixproj_body,
        grid=(n // TOKEN_TILE,),
        in_specs=[tile, _resident(norm.shape), _resident(w.shape), _resident(lb.shape)],
        out_specs=[out_tile] * len(MIXPROJ_DTYPES),
        out_shape=[jax.ShapeDtypeStruct((n, PROJ_WIDTH), dt) for dt in MIXPROJ_DTYPES],
        compiler_params=_params("parallel"),
        name="mixproj",
    )(h, norm, w, lb)


def _rel_bucket(rel):
    nb = N_BUCKETS // 2
    max_exact = nb // 2
    side = jnp.where(rel > 0, nb, 0)
    n = jnp.abs(rel)
    nf = jnp.maximum(n, 1).astype(jnp.float32)
    large = max_exact + (jnp.log(nf / max_exact) / math.log(MAX_DISTANCE / max_exact)
                         * (nb - max_exact)).astype(jnp.int32)
    large = jnp.minimum(large, nb - 1)
    return side + jnp.where(n < max_exact, n, large)


def _bias_tables(rel_bias, seq):
    nb = N_BUCKETS // 2
    n_q = seq // Q_TILE
    assert 2 * n_q <= HEAD_DIM and KEY_CHUNK == Q_TILE and NEAR_REACH * KEY_CHUNK >= MAX_DISTANCE
    rb = rel_bias.astype(F32) * LOG2E
    width = (2 * NEAR_REACH + 3) * KEY_CHUNK
    own = (NEAR_REACH + 1) * KEY_CHUNK
    i = jnp.arange(Q_TILE, dtype=jnp.int32)[:, None]
    j = jnp.arange(width, dtype=jnp.int32)[None, :]
    bucket = _rel_bucket(j - own - i)
    band = jnp.zeros((ATTN_HEADS, Q_TILE, width), F32)
    for n in range(N_BUCKETS):
        band = jnp.where(bucket[None] == n, rb[n][:, None, None], band)
    before = rb[nb - 1][:, None, None]
    after = rb[2 * nb - 1][:, None, None]
    near = band - jnp.where(j[None] < own, before, after)
    key_pos = jnp.arange(seq, dtype=jnp.int32)[None, :, None]
    tile_start = (jnp.arange(n_q, dtype=jnp.int32) * Q_TILE)[None, None, :]
    far = jnp.where(key_pos < tile_start, before, after)
    hi = far.astype(BF16)
    lo = (far - hi.astype(F32)).astype(BF16)
    pad = jnp.zeros((ATTN_HEADS, seq, HEAD_DIM - 2 * n_q), BF16)
    return jnp.concatenate([hi, lo, pad], axis=-1), near


N_ATTN_UNITS = 2 * Q_TILES_PER_STEP


def _attn_body(lam_ref, q_ref, k_ref, v_ref, far_ref, near_ref, norm_ref, o_ref,
               kx_ref, vx_ref, qx_ref, m_ref, *unit_refs):
    step = pl.program_id(2)
    seq = k_ref.shape[1]
    n_q = seq // Q_TILE
    n_kc = seq // KEY_CHUNK
    s_refs = unit_refs[:N_ATTN_UNITS]
    p_refs = unit_refs[N_ATTN_UNITS:]

    @pl.when(step == 0)
    def _():
        kx_ref[:, :HEAD_DIM] = k_ref[0]
        kx_ref[:, HEAD_DIM:] = far_ref[0]
        vx_ref[:, :HEAD_DIM] = v_ref[0]
        vx_ref[:, HEAD_DIM:] = jnp.ones((seq, HEAD_DIM), BF16)

    lane = lax.broadcasted_iota(jnp.int32, (Q_TILE, HEAD_DIM), 1)
    zero = jnp.zeros((Q_TILE, HEAD_DIM), BF16)
    for t in range(Q_TILES_PER_STEP):
        q = q_ref[0, t * Q_TILE:(t + 1) * Q_TILE, :]
        tile = step * Q_TILES_PER_STEP + t
        tile_onehot = jnp.where((lane == tile) | (lane == n_q + tile), 1.0, 0.0).astype(BF16)
        qx_ref[2 * t, :, :HEAD_DIM] = jnp.where(lane < ATTN_HALF_DIM, q, zero)
        qx_ref[2 * t + 1, :, :HEAD_DIM] = jnp.where(lane >= ATTN_HALF_DIM, q, zero)
        qx_ref[2 * t, :, HEAD_DIM:] = tile_onehot
        qx_ref[2 * t + 1, :, HEAD_DIM:] = tile_onehot
    m_ref[...] = jnp.full(m_ref.shape, -jnp.inf, F32)

    chunks_per_iter = min(CHUNKS_PER_ITER, n_kc)
    assert n_kc % chunks_per_iter == 0

    def score_chunks(it, carry):
        for j in range(chunks_per_iter):
            kc = it * chunks_per_iter + j
            cols = pl.ds(pl.multiple_of(kc * KEY_CHUNK, KEY_CHUNK), KEY_CHUNK)
            for t in range(Q_TILES_PER_STEP):
                rel = jnp.clip(kc - (step * Q_TILES_PER_STEP + t), -NEAR_REACH - 1, NEAR_REACH + 1)
                near = near_ref[0, :, pl.ds(pl.multiple_of((rel + NEAR_REACH + 1) * KEY_CHUNK, KEY_CHUNK),
                                            KEY_CHUNK)]
                for u in (2 * t, 2 * t + 1):
                    s = lax.dot_general(qx_ref[u], kx_ref[cols, :], NT_DIMS,
                                        preferred_element_type=F32) + near
                    s_refs[u][:, cols] = s
                    m_ref[u] = jnp.maximum(m_ref[u], jnp.maximum(s[:, :LANES], s[:, LANES:]))
        return carry
    lax.fori_loop(0, n_kc // chunks_per_iter, score_chunks, 0)

    outs = []
    for u in range(N_ATTN_UNITS):
        m = jnp.max(m_ref[u], axis=-1, keepdims=True)
        for kc in range(n_kc):
            cols = slice(kc * KEY_CHUNK, (kc + 1) * KEY_CHUNK)
            p_refs[u][:, cols] = jnp.exp2((s_refs[u][:, cols] - m).astype(BF16))
        acc = _dot(p_refs[u][...], vx_ref[...])
        outs.append(acc[:, :HEAD_DIM] / acc[:, HEAD_DIM:])
    for t in range(Q_TILES_PER_STEP):
        o = outs[2 * t] - lam_ref[0] * outs[2 * t + 1]
        o_ref[0, t * Q_TILE:(t + 1) * Q_TILE, :] = (
            _rms(o, norm_ref[...]) * (1.0 - LAMBDA_INIT)).astype(o_ref.dtype)


def _attention(q, k, v, far, near, lam, head_norm):
    b, s, _ = q.shape
    rows = Q_TILES_PER_STEP * Q_TILE
    smem = pl.BlockSpec(memory_space=pltpu.SMEM)
    return pl.pallas_call(
        _attn_body,
        grid=(b, ATTN_HEADS, s // rows),
        in_specs=[smem,
                  pl.BlockSpec((1, rows, HEAD_DIM), lambda bi, h, i: (bi, i, h)),
                  pl.BlockSpec((1, s, HEAD_DIM), lambda bi, h, i: (bi, 0, h)),
                  pl.BlockSpec((1, s, HEAD_DIM), lambda bi, h, i: (bi, 0, h)),
                  pl.BlockSpec((1, s, HEAD_DIM), lambda bi, h, i: (h, 0, 0)),
                  pl.BlockSpec((1,) + near.shape[1:], lambda bi, h, i: (h, 0, 0)),
                  _resident(head_norm.shape)],
        out_specs=pl.BlockSpec((1, rows, HEAD_DIM), lambda bi, h, i: (bi, i, h)),
        out_shape=jax.ShapeDtypeStruct((b, s, ATTN_WIDTH), BF16),
        scratch_shapes=[pltpu.VMEM((s, 2 * HEAD_DIM), BF16), pltpu.VMEM((s, 2 * HEAD_DIM), BF16),
                        pltpu.VMEM((N_ATTN_UNITS, Q_TILE, 2 * HEAD_DIM), BF16),
                        pltpu.VMEM((N_ATTN_UNITS, Q_TILE, LANES), F32)]
                       + [pltpu.VMEM((Q_TILE, s), F32)] * N_ATTN_UNITS
                       + [pltpu.VMEM((Q_TILE, s), BF16)] * N_ATTN_UNITS,
        compiler_params=_params("parallel", "parallel", "arbitrary"),
        name="attn",
    )(lam, q, k, v, far, near, head_norm)


def _chunk_triangles():
    t = np.arange(RNN_GROUP)[:, None]
    s = np.arange(RNN_GROUP)[None, :]
    same = (t // CHUNK) == (s // CHUNK)
    return np.stack([same & (s <= t), same & (s >= t)]).astype(np.float32)


def _hgrn2_body(q_ref, v_ref, gate_ref, kf_ref, kb_ref, gf_ref, gb_ref, tri_ref, norm_ref, o_ref,
                acc_ref, qd_ref, qm_ref, km_ref, kd_ref, u_ref, dl_ref, sp_ref):
    seq = q_ref.shape[1]
    n_groups = seq // RNN_GROUP
    n_chunks = seq // CHUNK
    per_group = RNN_GROUP // CHUNK
    k_refs = (kf_ref, kb_ref)
    g_refs = (gf_ref, gb_ref)
    mid_row = (CHUNK // 2 - 1, CHUNK // 2)
    last_row = (CHUNK - 1, 0)

    def chunked(t):
        return t.reshape(per_group, CHUNK, HEAD_DIM)

    def flat(t):
        return t.reshape(RNN_GROUP, HEAD_DIM)

    def group_rows(gi):
        return pl.ds(pl.multiple_of(gi * RNN_GROUP, RNN_GROUP), RNN_GROUP)

    def decays(gi):
        rows = group_rows(gi)
        q = chunked(q_ref[0, rows, :].astype(F32))
        for d in range(2):
            log_f = g_refs[d][0, rows, :]
            hi = log_f.astype(BF16)
            lo = (log_f - hi.astype(F32)).astype(BF16)
            both = _dot(tri_ref[d], jnp.concatenate([hi, lo], axis=1))
            cum = chunked(both[:, :HEAD_DIM] + both[:, HEAD_DIM:])
            cum_mid = cum[:, mid_row[d]:mid_row[d] + 1, :]
            cum_last = cum[:, last_row[d]:last_row[d] + 1, :]
            k = chunked(k_refs[d][0, rows, :].astype(F32))
            q_mid = q * jnp.exp(cum - cum_mid)
            k_mid = k * jnp.exp(cum_mid - cum)
            lanes = slice(d * HEAD_DIM, (d + 1) * HEAD_DIM)
            qm_ref[d, rows, :] = flat(q_mid).astype(BF16)
            km_ref[d, rows, :] = flat(k_mid).astype(BF16)
            qd_ref[rows, lanes] = flat(q_mid * jnp.exp(cum_mid)).astype(BF16)
            kd_ref[rows, lanes] = flat(k_mid * jnp.exp(cum_last - cum_mid)).astype(BF16)
            decay = jnp.exp(cum_last)
            for j in range(per_group):
                dl_ref[d, gi * per_group + j] = decay[j]

    def products(gi):
        rows = group_rows(gi)
        v = v_ref[0, rows, :]
        a = None
        for d in range(2):
            a_d = lax.dot_general(qm_ref[d, rows, :], km_ref[d, rows, :], NT_DIMS,
                                  preferred_element_type=F32).astype(BF16)
            a_d = jnp.where(tri_ref[d] > 0, a_d, jnp.zeros_like(a_d))
            a = a_d if a is None else a + a_d
        acc_ref[rows, :] = _dot(a, v)
        k_dec = kd_ref[rows, :]
        for j in range(per_group):
            chunk = slice(j * CHUNK, (j + 1) * CHUNK)
            u = lax.dot_general(v[chunk], k_dec[chunk], TN_DIMS, preferred_element_type=F32)
            for d in range(2):
                u_ref[d, gi * per_group + j] = u[:, d * HEAD_DIM:(d + 1) * HEAD_DIM]

    decays(0)

    def chunk_local(gi, carry):
        products(gi - 1)
        decays(gi)
        return carry
    lax.fori_loop(1, n_groups, chunk_local, 0, unroll=3)
    products(n_groups - 1)

    def scan(i, states):
        new = []
        for d, state in enumerate(states):
            ci = i if d == 0 else n_chunks - 1 - i
            sp_ref[ci, :, d * HEAD_DIM:(d + 1) * HEAD_DIM] = state.astype(BF16)
            new.append(state * dl_ref[d, ci] + u_ref[d, ci])
        return tuple(new)
    zero = jnp.zeros((HEAD_DIM, HEAD_DIM), F32)
    lax.fori_loop(0, n_chunks, scan, (zero, zero))

    def outputs(gi, carry):
        rows = group_rows(gi)
        entering = [lax.dot_general(qd_ref[pl.ds(pl.multiple_of(gi * RNN_GROUP + j * CHUNK, CHUNK), CHUNK), :],
                                    sp_ref[gi * per_group + j], NT_DIMS,
                                    preferred_element_type=F32) for j in range(per_group)]
        total = acc_ref[rows, :] + jnp.concatenate(entering, axis=0)
        gate = gate_ref[0, rows, :].astype(F32)
        o_ref[0, rows, :] = (_rms(total, norm_ref[...]) * gate).astype(o_ref.dtype)
        return carry
    lax.fori_loop(0, n_groups, outputs, 0, unroll=4)


def _hgrn2(q, v, gate, k_fw, k_bw, g_fw, g_bw, head_norm):
    b, s, _ = q.shape
    n_chunks = s // CHUNK
    tri = jnp.asarray(_chunk_triangles(), BF16)
    seq_block = pl.BlockSpec((1, s, HEAD_DIM), lambda bi, h: (bi, 0, h))
    return pl.pallas_call(
        _hgrn2_body,
        grid=(b, RNN_HEADS),
        in_specs=[seq_block] * 7 + [_resident(tri.shape), _resident(head_norm.shape)],
        out_specs=seq_block,
        out_shape=jax.ShapeDtypeStruct((b, s, RNN_WIDTH), BF16),
        scratch_shapes=[pltpu.VMEM((s, HEAD_DIM), F32),
                        pltpu.VMEM((s, 2 * HEAD_DIM), BF16),
                        pltpu.VMEM((2, s, HEAD_DIM), BF16),
                        pltpu.VMEM((2, s, HEAD_DIM), BF16),
                        pltpu.VMEM((s, 2 * HEAD_DIM), BF16),
                        pltpu.VMEM((2, n_chunks, HEAD_DIM, HEAD_DIM), F32),
                        pltpu.VMEM((2, n_chunks, 1, HEAD_DIM), F32),
                        pltpu.VMEM((n_chunks, HEAD_DIM, 2 * HEAD_DIM), BF16)],
        compiler_params=_params("parallel", "parallel"),
        name="hgrn2",
    )(q, v, gate, k_fw, k_bw, g_fw, g_bw, tri, head_norm)


def kernel(x, rel_bias, lb_logits, ffn1_pre_norm, ffn1_w_in, ffn1_w_out, ffn1_post_norm, mix_pre_norm, w_mix_in, lambda_q1, lambda_k1, lambda_q2, lambda_k2, attn_head_norm, rnn_head_norm, w_mix_out, mix_post_norm, ffn2_pre_norm, ffn2_w_in, ffn2_w_out, ffn2_post_norm):
    b, s, d = x.shape
    layer = 0
    row = lambda t: t[layer].reshape(1, -1).astype(F32)

    h1 = _ffn(x.reshape(b * s, d), row(ffn1_pre_norm), ffn1_w_in[layer].astype(BF16),
              ffn1_w_out[layer].astype(BF16), row(ffn1_post_norm))

    lb = jnp.cumsum(jax.nn.softmax(lb_logits.astype(F32), axis=1), axis=1)[:, layer]
    q_a, k_a, v_a, q_r, i_r, g_r, k_fw, k_bw, g_fw, g_bw = _mixproj(
        h1, row(mix_pre_norm), w_mix_in[layer].astype(BF16), lb.reshape(2, 1, RNN_WIDTH))
    seq = lambda t: t.reshape(b, s, t.shape[-1])

    lam = (jnp.exp(jnp.sum(lambda_q1[layer] * lambda_k1[layer]).astype(F32))
           - jnp.exp(jnp.sum(lambda_q2[layer] * lambda_k2[layer]).astype(F32))
           + LAMBDA_INIT).reshape(1)
    far, near = _bias_tables(rel_bias, s)
    o_a = _attention(seq(q_a), seq(k_a), seq(v_a), far, near, lam, row(attn_head_norm))

    o_r = _hgrn2(seq(q_r), seq(i_r), seq(g_r), seq(k_fw), seq(k_bw), seq(g_fw), seq(g_bw),
                 row(rnn_head_norm))

    y = _mix_ffn(h1, o_a.reshape(b * s, ATTN_WIDTH), o_r.reshape(b * s, RNN_WIDTH),
                 w_mix_out[layer].astype(BF16), row(mix_post_norm), row(ffn2_pre_norm),
                 ffn2_w_in[layer].astype(BF16), ffn2_w_out[layer].astype(BF16),
                 row(ffn2_post_norm))
    return y.reshape(b, s, d).astype(x.dtype)
```

```python
import functools
import math

import numpy as np
import jax
import jax.numpy as jnp
from jax import lax
from jax.experimental import pallas as pl
from jax.experimental.pallas import tpu as pltpu

F32 = jnp.float32
BF16 = jnp.bfloat16

D_MODEL = 1024
ATTN_WIDTH = 512
RNN_WIDTH = 512
ATTN_HEADS = 4
ATTN_HALF_DIM = 64
HEAD_DIM = 128
RNN_HEADS = 4
D_FF = 2816
N_BUCKETS = 32
MAX_DISTANCE = 128
CHUNK = 64
EPS = 1e-6
LAMBDA_INIT = 0.8 - 0.6 * math.exp(-0.3 * 0)

VMEM_LIMIT_BYTES = 56 * 1024 * 1024
LANES = 128

TOKEN_TILE = 512
FF_CHUNK = 256
Q_TILE = 256
Q_TILES_PER_STEP = 2
KEY_CHUNK = 256
CHUNKS_PER_ITER = 8
NEAR_REACH = 1
LOG2E = math.log2(math.e)
RNN_GROUP = 256

NT_DIMS = (((1,), (1,)), ((), ()))
TN_DIMS = (((0,), (0,)), ((), ()))


def _rms(x, w):
    return x * lax.rsqrt(jnp.mean(x * x, axis=-1, keepdims=True) + EPS) * w


def _dot(a, b):
    return jnp.dot(a, b, preferred_element_type=F32)


def _resident(shape):
    return pl.BlockSpec(shape, lambda *_: (0,) * len(shape), pipeline_mode=pl.Buffered(1))


def _params(*semantics):
    return pltpu.CompilerParams(dimension_semantics=semantics, vmem_limit_bytes=VMEM_LIMIT_BYTES)


def _swiglu_half_step(h, pre_ref, w_in_ref, w_out_ref, post_ref, acc_ref):
    u = _rms(h, pre_ref[...]).astype(BF16)
    for c in range(D_FF // FF_CHUNK):
        lo = c * FF_CHUNK
        gate = _dot(u, w_in_ref[:, lo:lo + FF_CHUNK])
        up = _dot(u, w_in_ref[:, D_FF + lo:D_FF + lo + FF_CHUNK])
        act = (gate * jax.nn.sigmoid(gate) * up).astype(BF16)
        part = _dot(act, w_out_ref[lo:lo + FF_CHUNK, :])
        if c == 0:
            acc_ref[...] = part
        else:
            acc_ref[...] += part
    return h + 0.5 * _rms(acc_ref[...], post_ref[...])


def _ffn_body(x_ref, pre_ref, w_in_ref, w_out_ref, post_ref, o_ref, acc_ref):
    o_ref[...] = _swiglu_half_step(x_ref[...], pre_ref, w_in_ref, w_out_ref, post_ref, acc_ref)


def _ffn(x, pre, w_in, w_out, post):
    n, d = x.shape
    tile = pl.BlockSpec((TOKEN_TILE, d), lambda i: (i, 0))
    return pl.pallas_call(
        _ffn_body,
        grid=(n // TOKEN_TILE,),
        in_specs=[tile, _resident(pre.shape), _resident(w_in.shape), _resident(w_out.shape),
                  _resident(post.shape)],
        out_specs=tile,
        out_shape=jax.ShapeDtypeStruct((n, d), F32),
        scratch_shapes=[pltpu.VMEM((TOKEN_TILE, d), F32)],
        compiler_params=_params("parallel"),
        name="ffn",
    )(x, pre, w_in, w_out, post)


def _mix_ffn_body(h_ref, oa_ref, or_ref, w_mix_ref, mix_post_ref, pre_ref, w_in_ref, w_out_ref,
                  post_ref, o_ref, acc_ref):
    def heads(ref):
        return jnp.concatenate([ref[0, hd] for hd in range(ref.shape[1])], axis=1)

    mixed = (_dot(heads(oa_ref), w_mix_ref[:ATTN_WIDTH, :])
             + _dot(heads(or_ref), w_mix_ref[ATTN_WIDTH:, :]))
    h = h_ref[...] + _rms(mixed, mix_post_ref[...])
    o_ref[...] = _swiglu_half_step(h, pre_ref, w_in_ref, w_out_ref, post_ref, acc_ref)


def _head_major_tile(seq):
    tiles_per_seq = seq // TOKEN_TILE
    return pl.BlockSpec((1, ATTN_HEADS, TOKEN_TILE, HEAD_DIM),
                        lambda i: (i // tiles_per_seq, 0, i % tiles_per_seq, 0))


def _mix_ffn(h, o_a, o_r, w_mix, mix_post, pre, w_in, w_out, post):
    n, d = h.shape
    tile = pl.BlockSpec((TOKEN_TILE, d), lambda i: (i, 0))
    half = _head_major_tile(o_a.shape[2])
    return pl.pallas_call(
        _mix_ffn_body,
        grid=(n // TOKEN_TILE,),
        in_specs=[tile, half, half, _resident(w_mix.shape), _resident(mix_post.shape),
                  _resident(pre.shape), _resident(w_in.shape), _resident(w_out.shape),
                  _resident(post.shape)],
        out_specs=tile,
        out_shape=jax.ShapeDtypeStruct((n, d), F32),
        scratch_shapes=[pltpu.VMEM((TOKEN_TILE, d), F32)],
        compiler_params=_params("parallel"),
        name="mix_ffn",
    )(h, o_a, o_r, w_mix, mix_post, pre, w_in, w_out, post)


PROJ_WIDTH = 512
MIXPROJ_DTYPES = [BF16] * 8 + [F32] * 2


def _sigmoid(x):
    return 0.5 * jnp.tanh(0.5 * x) + 0.5


def _silu(x):
    return x * _sigmoid(x)


def _mixproj_body(h_ref, norm_ref, w_ref, lb_ref, qa_ref, ka_ref, va_ref, qr_ref, ir_ref, gr_ref,
                  kf_ref, kb_ref, gf_ref, gb_ref):
    u = _rms(h_ref[...], norm_ref[...]).astype(BF16)

    def proj(j):
        return _dot(u, w_ref[:, j * PROJ_WIDTH:(j + 1) * PROJ_WIDTH])

    def put(o_ref, val):
        for hd in range(o_ref.shape[1]):
            o_ref[0, hd] = val[:, hd * HEAD_DIM:(hd + 1) * HEAD_DIM].astype(o_ref.dtype)

    put(qa_ref, proj(0) * (ATTN_HALF_DIM ** -0.5 * LOG2E))
    put(ka_ref, proj(1))
    put(va_ref, proj(2))
    put(qr_ref, _silu(proj(3)))
    put(ir_ref, proj(4))
    for direction, (k_ref, g_ref) in enumerate(((kf_ref, gf_ref), (kb_ref, gb_ref))):
        lb = lb_ref[direction]
        sig = _sigmoid(proj(5 + direction))
        put(k_ref, (1.0 - lb) * (1.0 - sig))
        put(g_ref, jnp.log(lb + (1.0 - lb) * sig))
    put(gr_ref, _silu(proj(7)))


def _mixproj(h, norm, w, lb, batch):
    n, d = h.shape
    seq = n // batch
    tile = pl.BlockSpec((TOKEN_TILE, d), lambda i: (i, 0))
    return pl.pallas_call(
        _mixproj_body,
        grid=(n // TOKEN_TILE,),
        in_specs=[tile, _resident(norm.shape), _resident(w.shape), _resident(lb.shape)],
        out_specs=[_head_major_tile(seq)] * len(MIXPROJ_DTYPES),
        out_shape=[jax.ShapeDtypeStruct((batch, ATTN_HEADS, seq, HEAD_DIM), dt) for dt in MIXPROJ_DTYPES],
        compiler_params=_params("parallel"),
        name="mixproj",
    )(h, norm, w, lb)


def _rel_bucket(rel):
    nb = N_BUCKETS // 2
    max_exact = nb // 2
    side = jnp.where(rel > 0, nb, 0)
    n = jnp.abs(rel)
    nf = jnp.maximum(n, 1).astype(jnp.float32)
    large = max_exact + (jnp.log(nf / max_exact) / math.log(MAX_DISTANCE / max_exact)
                         * (nb - max_exact)).astype(jnp.int32)
    large = jnp.minimum(large, nb - 1)
    return side + jnp.where(n < max_exact, n, large)


def _bias_tables(rel_bias, seq):
    nb = N_BUCKETS // 2
    n_q = seq // Q_TILE
    assert 2 * n_q <= HEAD_DIM and KEY_CHUNK == Q_TILE and NEAR_REACH * KEY_CHUNK >= MAX_DISTANCE
    rb = rel_bias.astype(F32) * LOG2E
    width = (2 * NEAR_REACH + 3) * KEY_CHUNK
    own = (NEAR_REACH + 1) * KEY_CHUNK
    i = jnp.arange(Q_TILE, dtype=jnp.int32)[:, None]
    j = jnp.arange(width, dtype=jnp.int32)[None, :]
    bucket = _rel_bucket(j - own - i)
    band = jnp.zeros((ATTN_HEADS, Q_TILE, width), F32)
    for n in range(N_BUCKETS):
        band = jnp.where(bucket[None] == n, rb[n][:, None, None], band)
    before = rb[nb - 1][:, None, None]
    after = rb[2 * nb - 1][:, None, None]
    near = band - jnp.where(j[None] < own, before, after)
    key_pos = jnp.arange(seq, dtype=jnp.int32)[None, :, None]
    tile_start = (jnp.arange(n_q, dtype=jnp.int32) * Q_TILE)[None, None, :]
    far = jnp.where(key_pos < tile_start, before, after)
    hi = far.astype(BF16)
    lo = (far - hi.astype(F32)).astype(BF16)
    pad = jnp.zeros((ATTN_HEADS, seq, HEAD_DIM - 2 * n_q), BF16)
    return jnp.concatenate([hi, lo, pad], axis=-1), near


N_ATTN_UNITS = 2 * Q_TILES_PER_STEP


def _attn_body(lam_ref, q_ref, k_ref, v_ref, far_ref, near_ref, norm_ref, o_ref,
               kx_ref, vx_ref, qx_ref, m_ref, *unit_refs):
    step = pl.program_id(2)
    seq = k_ref.shape[1]
    n_q = seq // Q_TILE
    n_kc = seq // KEY_CHUNK
    s_refs = unit_refs[:N_ATTN_UNITS]
    p_refs = unit_refs[N_ATTN_UNITS:]

    @pl.when(step == 0)
    def _():
        kx_ref[:, :HEAD_DIM] = k_ref[0]
        kx_ref[:, HEAD_DIM:] = far_ref[0]
        vx_ref[:, :HEAD_DIM] = v_ref[0]
        vx_ref[:, HEAD_DIM:] = jnp.ones((seq, HEAD_DIM), BF16)

    lane = lax.broadcasted_iota(jnp.int32, (Q_TILE, HEAD_DIM), 1)
    zero = jnp.zeros((Q_TILE, HEAD_DIM), BF16)
    for t in range(Q_TILES_PER_STEP):
        q = q_ref[0, t * Q_TILE:(t + 1) * Q_TILE, :]
        tile = step * Q_TILES_PER_STEP + t
        tile_onehot = jnp.where((lane == tile) | (lane == n_q + tile), 1.0, 0.0).astype(BF16)
        qx_ref[2 * t, :, :HEAD_DIM] = jnp.where(lane < ATTN_HALF_DIM, q, zero)
        qx_ref[2 * t + 1, :, :HEAD_DIM] = jnp.where(lane >= ATTN_HALF_DIM, q, zero)
        qx_ref[2 * t, :, HEAD_DIM:] = tile_onehot
        qx_ref[2 * t + 1, :, HEAD_DIM:] = tile_onehot
    m_ref[...] = jnp.full(m_ref.shape, -jnp.inf, F32)

    chunks_per_iter = min(CHUNKS_PER_ITER, n_kc)
    assert n_kc % chunks_per_iter == 0

    def score_chunks(it, carry):
        for j in range(chunks_per_iter):
            kc = it * chunks_per_iter + j
            cols = pl.ds(pl.multiple_of(kc * KEY_CHUNK, KEY_CHUNK), KEY_CHUNK)
            for t in range(Q_TILES_PER_STEP):
                rel = jnp.clip(kc - (step * Q_TILES_PER_STEP + t), -NEAR_REACH - 1, NEAR_REACH + 1)
                near = near_ref[0, :, pl.ds(pl.multiple_of((rel + NEAR_REACH + 1) * KEY_CHUNK, KEY_CHUNK),
                                            KEY_CHUNK)]
                for u in (2 * t, 2 * t + 1):
                    s = lax.dot_general(qx_ref[u], kx_ref[cols, :], NT_DIMS,
                                        preferred_element_type=F32) + near
                    s_refs[u][:, cols] = s
                    m_ref[u] = jnp.maximum(m_ref[u], jnp.maximum(s[:, :LANES], s[:, LANES:]))
        return carry
    lax.fori_loop(0, n_kc // chunks_per_iter, score_chunks, 0)

    outs = []
    for u in range(N_ATTN_UNITS):
        m = jnp.max(m_ref[u], axis=-1, keepdims=True)
        for kc in range(n_kc):
            cols = slice(kc * KEY_CHUNK, (kc + 1) * KEY_CHUNK)
            p_refs[u][:, cols] = jnp.exp2((s_refs[u][:, cols] - m).astype(BF16))
        acc = _dot(p_refs[u][...], vx_ref[...])
        outs.append(acc[:, :HEAD_DIM] / acc[:, HEAD_DIM:])
    for t in range(Q_TILES_PER_STEP):
        o = outs[2 * t] - lam_ref[0] * outs[2 * t + 1]
        o_ref[0, t * Q_TILE:(t + 1) * Q_TILE, :] = (
            _rms(o, norm_ref[...]) * (1.0 - LAMBDA_INIT)).astype(o_ref.dtype)


def _attention(q, k, v, far, near, lam, head_norm):
    b, _, s, _ = q.shape
    rows = Q_TILES_PER_STEP * Q_TILE
    smem = pl.BlockSpec(memory_space=pltpu.SMEM)
    head_seq = pl.BlockSpec((None, 1, s, HEAD_DIM), lambda bi, h, i: (bi, h, 0, 0))
    head_rows = pl.BlockSpec((None, 1, rows, HEAD_DIM), lambda bi, h, i: (bi, h, i, 0))
    return pl.pallas_call(
        _attn_body,
        grid=(b, ATTN_HEADS, s // rows),
        in_specs=[smem, head_rows, head_seq, head_seq,
                  pl.BlockSpec((1, s, HEAD_DIM), lambda bi, h, i: (h, 0, 0)),
                  pl.BlockSpec((1,) + near.shape[1:], lambda bi, h, i: (h, 0, 0)),
                  _resident(head_norm.shape)],
        out_specs=head_rows,
        out_shape=jax.ShapeDtypeStruct((b, ATTN_HEADS, s, HEAD_DIM), BF16),
        scratch_shapes=[pltpu.VMEM((s, 2 * HEAD_DIM), BF16), pltpu.VMEM((s, 2 * HEAD_DIM), BF16),
                        pltpu.VMEM((N_ATTN_UNITS, Q_TILE, 2 * HEAD_DIM), BF16),
                        pltpu.VMEM((N_ATTN_UNITS, Q_TILE, LANES), F32)]
                       + [pltpu.VMEM((Q_TILE, s), F32)] * N_ATTN_UNITS
                       + [pltpu.VMEM((Q_TILE, s), BF16)] * N_ATTN_UNITS,
        compiler_params=_params("parallel", "parallel", "arbitrary"),
        name="attn",
    )(lam, q, k, v, far, near, head_norm)


def _chunk_triangles():
    t = np.arange(RNN_GROUP)[:, None]
    s = np.arange(RNN_GROUP)[None, :]
    same = (t // CHUNK) == (s // CHUNK)
    return np.stack([same & (s <= t), same & (s >= t)]).astype(np.float32)


def _hgrn2_body(q_ref, v_ref, gate_ref, kf_ref, kb_ref, gf_ref, gb_ref, tri_ref, norm_ref, o_ref,
                acc_ref, qd_ref, qm_ref, km_ref, kd_ref, u_ref, dl_ref, sp_ref):
    seq = q_ref.shape[1]
    n_groups = seq // RNN_GROUP
    n_chunks = seq // CHUNK
    per_group = RNN_GROUP // CHUNK
    k_refs = (kf_ref, kb_ref)
    g_refs = (gf_ref, gb_ref)
    mid_row = (CHUNK // 2 - 1, CHUNK // 2)
    last_row = (CHUNK - 1, 0)

    def chunked(t):
        return t.reshape(per_group, CHUNK, HEAD_DIM)

    def flat(t):
        return t.reshape(RNN_GROUP, HEAD_DIM)

    def group_rows(gi):
        return pl.ds(pl.multiple_of(gi * RNN_GROUP, RNN_GROUP), RNN_GROUP)

    def decays(gi):
        rows = group_rows(gi)
        q = chunked(q_ref[0, rows, :].astype(F32))
        for d in range(2):
            log_f = g_refs[d][0, rows, :]
            hi = log_f.astype(BF16)
            lo = (log_f - hi.astype(F32)).astype(BF16)
            both = _dot(tri_ref[d], jnp.concatenate([hi, lo], axis=1))
            cum = chunked(both[:, :HEAD_DIM] + both[:, HEAD_DIM:])
            cum_mid = cum[:, mid_row[d]:mid_row[d] + 1, :]
            cum_last = cum[:, last_row[d]:last_row[d] + 1, :]
            k = chunked(k_refs[d][0, rows, :].astype(F32))
            q_mid = q * jnp.exp(cum - cum_mid)
            k_mid = k * jnp.exp(cum_mid - cum)
            lanes = slice(d * HEAD_DIM, (d + 1) * HEAD_DIM)
            qm_ref[d, rows, :] = flat(q_mid).astype(BF16)
            km_ref[d, rows, :] = flat(k_mid).astype(BF16)
            qd_ref[rows, lanes] = flat(q_mid * jnp.exp(cum_mid)).astype(BF16)
            kd_ref[rows, lanes] = flat(k_mid * jnp.exp(cum_last - cum_mid)).astype(BF16)
            decay = jnp.exp(cum_last)
            for j in range(per_group):
                dl_ref[d, gi * per_group + j] = decay[j]

    def products(gi):
        rows = group_rows(gi)
        v = v_ref[0, rows, :]
        a = None
        for d in range(2):
            a_d = lax.dot_general(qm_ref[d, rows, :], km_ref[d, rows, :], NT_DIMS,
                                  preferred_element_type=F32).astype(BF16)
            a_d = jnp.where(tri_ref[d] > 0, a_d, jnp.zeros_like(a_d))
            a = a_d if a is None else a + a_d
        acc_ref[rows, :] = _dot(a, v)
        k_dec = kd_ref[rows, :]
        for j in range(per_group):
            chunk = slice(j * CHUNK, (j + 1) * CHUNK)
            u = lax.dot_general(v[chunk], k_dec[chunk], TN_DIMS, preferred_element_type=F32)
            for d in range(2):
                u_ref[d, gi * per_group + j] = u[:, d * HEAD_DIM:(d + 1) * HEAD_DIM]

    decays(0)

    def chunk_local(gi, carry):
        products(gi - 1)
        decays(gi)
        return carry
    lax.fori_loop(1, n_groups, chunk_local, 0, unroll=3)
    products(n_groups - 1)

    def scan(i, states):
        new = []
        for d, state in enumerate(states):
            ci = i if d == 0 else n_chunks - 1 - i
            sp_ref[ci, :, d * HEAD_DIM:(d + 1) * HEAD_DIM] = state.astype(BF16)
            new.append(state * dl_ref[d, ci] + u_ref[d, ci])
        return tuple(new)
    zero = jnp.zeros((HEAD_DIM, HEAD_DIM), F32)
    lax.fori_loop(0, n_chunks, scan, (zero, zero))

    def outputs(gi, carry):
        rows = group_rows(gi)
        entering = [lax.dot_general(qd_ref[pl.ds(pl.multiple_of(gi * RNN_GROUP + j * CHUNK, CHUNK), CHUNK), :],
                                    sp_ref[gi * per_group + j], NT_DIMS,
                                    preferred_element_type=F32) for j in range(per_group)]
        total = acc_ref[rows, :] + jnp.concatenate(entering, axis=0)
        gate = gate_ref[0, rows, :].astype(F32)
        o_ref[0, rows, :] = (_rms(total, norm_ref[...]) * gate).astype(o_ref.dtype)
        return carry
    lax.fori_loop(0, n_groups, outputs, 0, unroll=4)


def _hgrn2(q, v, gate, k_fw, k_bw, g_fw, g_bw, head_norm):
    b, _, s, _ = q.shape
    n_chunks = s // CHUNK
    tri = jnp.asarray(_chunk_triangles(), BF16)
    seq_block = pl.BlockSpec((None, 1, s, HEAD_DIM), lambda bi, h: (bi, h, 0, 0))
    return pl.pallas_call(
        _hgrn2_body,
        grid=(b, RNN_HEADS),
        in_specs=[seq_block] * 7 + [_resident(tri.shape), _resident(head_norm.shape)],
        out_specs=seq_block,
        out_shape=jax.ShapeDtypeStruct((b, RNN_HEADS, s, HEAD_DIM), BF16),
        scratch_shapes=[pltpu.VMEM((s, HEAD_DIM), F32),
                        pltpu.VMEM((s, 2 * HEAD_DIM), BF16),
                        pltpu.VMEM((2, s, HEAD_DIM), BF16),
                        pltpu.VMEM((2, s, HEAD_DIM), BF16),
                        pltpu.VMEM((s, 2 * HEAD_DIM), BF16),
                        pltpu.VMEM((2, n_chunks, HEAD_DIM, HEAD_DIM), F32),
                        pltpu.VMEM((2, n_chunks, 1, HEAD_DIM), F32),
                        pltpu.VMEM((n_chunks, HEAD_DIM, 2 * HEAD_DIM), BF16)],
        compiler_params=_params("parallel", "parallel"),
        name="hgrn2",
    )(q, v, gate, k_fw, k_bw, g_fw, g_bw, tri, head_norm)


def kernel(x, rel_bias, lb_logits, ffn1_pre_norm, ffn1_w_in, ffn1_w_out, ffn1_post_norm, mix_pre_norm, w_mix_in, lambda_q1, lambda_k1, lambda_q2, lambda_k2, attn_head_norm, rnn_head_norm, w_mix_out, mix_post_norm, ffn2_pre_norm, ffn2_w_in, ffn2_w_out, ffn2_post_norm):
    b, s, d = x.shape
    layer = 0
    row = lambda t: t[layer].reshape(1, -1).astype(F32)

    h1 = _ffn(x.reshape(b * s, d), row(ffn1_pre_norm), ffn1_w_in[layer].astype(BF16),
              ffn1_w_out[layer].astype(BF16), row(ffn1_post_norm))

    lb = jnp.cumsum(jax.nn.softmax(lb_logits.astype(F32), axis=1), axis=1)[:, layer]
    q_a, k_a, v_a, q_r, i_r, g_r, k_fw, k_bw, g_fw, g_bw = _mixproj(
        h1, row(mix_pre_norm), w_mix_in[layer].astype(BF16), lb.reshape(2, 1, RNN_WIDTH), b)

    lam = (jnp.exp(jnp.sum(lambda_q1[layer] * lambda_k1[layer]).astype(F32))
           - jnp.exp(jnp.sum(lambda_q2[layer] * lambda_k2[layer]).astype(F32))
           + LAMBDA_INIT).reshape(1)
    far, near = _bias_tables(rel_bias, s)
    o_a = _attention(q_a, k_a, v_a, far, near, lam, row(attn_head_norm))
    o_r = _hgrn2(q_r, i_r, g_r, k_fw, k_bw, g_fw, g_bw, row(rnn_head_norm))

    y = _mix_ffn(h1, o_a, o_r,
                 w_mix_out[layer].astype(BF16), row(mix_post_norm), row(ffn2_pre_norm),
                 ffn2_w_in[layer].astype(BF16), ffn2_w_out[layer].astype(BF16),
                 row(ffn2_post_norm))
    return y.reshape(b, s, d).astype(x.dtype)
```

```python
import functools
import math

import numpy as np
import jax
import jax.numpy as jnp
from jax import lax
from jax.experimental import pallas as pl
from jax.experimental.pallas import tpu as pltpu

F32 = jnp.float32
BF16 = jnp.bfloat16

D_MODEL = 1024
ATTN_WIDTH = 512
RNN_WIDTH = 512
ATTN_HEADS = 4
ATTN_HALF_DIM = 64
HEAD_DIM = 128
RNN_HEADS = 4
D_FF = 2816
N_BUCKETS = 32
MAX_DISTANCE = 128
CHUNK = 64
EPS = 1e-6
LAMBDA_INIT = 0.8 - 0.6 * math.exp(-0.3 * 0)

VMEM_LIMIT_BYTES = 56 * 1024 * 1024
LANES = 128

TOKEN_TILE = 512
MIXPROJ_TILE = 1024
FF_CHUNK = 256
Q_TILE = 256
Q_TILES_PER_STEP = 2
KEY_CHUNK = 256
CHUNKS_PER_ITER = 8
NEAR_REACH = 1
LOG2E = math.log2(math.e)
RNN_GROUP = 256

NT_DIMS = (((1,), (1,)), ((), ()))
TN_DIMS = (((0,), (0,)), ((), ()))


def _rms(x, w):
    return x * lax.rsqrt(jnp.mean(x * x, axis=-1, keepdims=True) + EPS) * w


def _dot(a, b):
    return jnp.dot(a, b, preferred_element_type=F32)


def _sigmoid(x):
    return 0.5 * jnp.tanh(0.5 * x) + 0.5


def _silu(x):
    return x * _sigmoid(x)


def _resident(shape):
    return pl.BlockSpec(shape, lambda *_: (0,) * len(shape), pipeline_mode=pl.Buffered(1))


def _params(*semantics):
    return pltpu.CompilerParams(dimension_semantics=semantics, vmem_limit_bytes=VMEM_LIMIT_BYTES)


def _swiglu_half_step(h, pre_ref, w_in_ref, w_out_ref, post_ref, acc_ref):
    u = _rms(h, pre_ref[...]).astype(BF16)
    for c in range(D_FF // FF_CHUNK):
        lo = c * FF_CHUNK
        gate = _dot(u, w_in_ref[:, lo:lo + FF_CHUNK])
        up = _dot(u, w_in_ref[:, D_FF + lo:D_FF + lo + FF_CHUNK])
        act = (_silu(gate) * up).astype(BF16)
        part = _dot(act, w_out_ref[lo:lo + FF_CHUNK, :])
        if c == 0:
            acc_ref[...] = part
        else:
            acc_ref[...] += part
    return h + 0.5 * _rms(acc_ref[...], post_ref[...])


def _ffn_body(x_ref, pre_ref, w_in_ref, w_out_ref, post_ref, o_ref, acc_ref):
    o_ref[...] = _swiglu_half_step(x_ref[...], pre_ref, w_in_ref, w_out_ref, post_ref, acc_ref)


def _ffn(x, pre, w_in, w_out, post):
    n, d = x.shape
    tile = pl.BlockSpec((TOKEN_TILE, d), lambda i: (i, 0))
    return pl.pallas_call(
        _ffn_body,
        grid=(n // TOKEN_TILE,),
        in_specs=[tile, _resident(pre.shape), _resident(w_in.shape), _resident(w_out.shape),
                  _resident(post.shape)],
        out_specs=tile,
        out_shape=jax.ShapeDtypeStruct((n, d), F32),
        scratch_shapes=[pltpu.VMEM((TOKEN_TILE, d), F32)],
        compiler_params=_params("parallel"),
        name="ffn",
    )(x, pre, w_in, w_out, post)


def _mix_ffn_body(h_ref, oa_ref, or_ref, w_mix_ref, mix_post_ref, pre_ref, w_in_ref, w_out_ref,
                  post_ref, o_ref, acc_ref):
    mixed = (_dot(oa_ref[...], w_mix_ref[:ATTN_WIDTH, :])
             + _dot(or_ref[...], w_mix_ref[ATTN_WIDTH:, :]))
    h = h_ref[...] + _rms(mixed, mix_post_ref[...])
    o_ref[...] = _swiglu_half_step(h, pre_ref, w_in_ref, w_out_ref, post_ref, acc_ref)


def _mix_ffn(h, o_a, o_r, w_mix, mix_post, pre, w_in, w_out, post):
    n, d = h.shape
    tile = pl.BlockSpec((TOKEN_TILE, d), lambda i: (i, 0))
    half = pl.BlockSpec((TOKEN_TILE, ATTN_WIDTH), lambda i: (i, 0))
    return pl.pallas_call(
        _mix_ffn_body,
        grid=(n // TOKEN_TILE,),
        in_specs=[tile, half, half, _resident(w_mix.shape), _resident(mix_post.shape),
                  _resident(pre.shape), _resident(w_in.shape), _resident(w_out.shape),
                  _resident(post.shape)],
        out_specs=tile,
        out_shape=jax.ShapeDtypeStruct((n, d), F32),
        scratch_shapes=[pltpu.VMEM((TOKEN_TILE, d), F32)],
        compiler_params=_params("parallel"),
        name="mix_ffn",
    )(h, o_a, o_r, w_mix, mix_post, pre, w_in, w_out, post)


PROJ_WIDTH = 512
MIXPROJ_DTYPES = [BF16] * 8 + [F32] * 2


def _mixproj_body(h_ref, norm_ref, w_ref, lb_ref, qa_ref, ka_ref, va_ref, qr_ref, ir_ref, gr_ref,
                  kf_ref, kb_ref, gf_ref, gb_ref):
    u = _rms(h_ref[...], norm_ref[...]).astype(BF16)

    def proj(j):
        return _dot(u, w_ref[:, j * PROJ_WIDTH:(j + 1) * PROJ_WIDTH])

    for direction, (k_ref, g_ref) in enumerate(((kf_ref, gf_ref), (kb_ref, gb_ref))):
        lb = lb_ref[direction]
        sig = _sigmoid(proj(5 + direction))
        k_ref[...] = ((1.0 - lb) * (1.0 - sig)).astype(BF16)
        g_ref[...] = jnp.log(lb + (1.0 - lb) * sig)
    qr_ref[...] = _silu(proj(3)).astype(BF16)
    gr_ref[...] = _silu(proj(7)).astype(BF16)
    qa_ref[...] = (proj(0) * (ATTN_HALF_DIM ** -0.5 * LOG2E)).astype(BF16)
    ka_ref[...] = proj(1).astype(BF16)
    va_ref[...] = proj(2).astype(BF16)
    ir_ref[...] = proj(4).astype(BF16)


def _mixproj(h, norm, w, lb):
    n, d = h.shape
    tile = pl.BlockSpec((MIXPROJ_TILE, d), lambda i: (i, 0))
    out_tile = pl.BlockSpec((MIXPROJ_TILE, PROJ_WIDTH), lambda i: (i, 0))
    return pl.pallas_call(
        _mixproj_body,
        grid=(n // MIXPROJ_TILE,),
        in_specs=[tile, _resident(norm.shape), _resident(w.shape), _resident(lb.shape)],
        out_specs=[out_tile] * len(MIXPROJ_DTYPES),
        out_shape=[jax.ShapeDtypeStruct((n, PROJ_WIDTH), dt) for dt in MIXPROJ_DTYPES],
        compiler_params=_params("parallel"),
        name="mixproj",
    )(h, norm, w, lb)


def _rel_bucket(rel):
    nb = N_BUCKETS // 2
    max_exact = nb // 2
    side = jnp.where(rel > 0, nb, 0)
    n = jnp.abs(rel)
    nf = jnp.maximum(n, 1).astype(jnp.float32)
    large = max_exact + (jnp.log(nf / max_exact) / math.log(MAX_DISTANCE / max_exact)
                         * (nb - max_exact)).astype(jnp.int32)
    large = jnp.minimum(large, nb - 1)
    return side + jnp.where(n < max_exact, n, large)


def _bias_tables(rel_bias, seq):
    nb = N_BUCKETS // 2
    n_q = seq // Q_TILE
    assert 2 * n_q <= HEAD_DIM and KEY_CHUNK == Q_TILE and NEAR_REACH * KEY_CHUNK >= MAX_DISTANCE
    rb = rel_bias.astype(F32) * LOG2E
    width = (2 * NEAR_REACH + 3) * KEY_CHUNK
    own = (NEAR_REACH + 1) * KEY_CHUNK
    span = width + Q_TILE - 1
    bucket = _rel_bucket(jnp.arange(span, dtype=jnp.int32) - (own + Q_TILE - 1))
    line = jnp.zeros((ATTN_HEADS, span), F32)
    for n in range(N_BUCKETS):
        line = jnp.where(bucket[None] == n, rb[n][:, None], line)
    line = jnp.pad(line, ((0, 0), (0, 1)))
    skew = jnp.broadcast_to(line[:, None, :], (ATTN_HEADS, Q_TILE, span + 1))
    skew = skew.reshape(ATTN_HEADS, -1)[:, :Q_TILE * span].reshape(ATTN_HEADS, Q_TILE, span)
    band = skew[:, :, Q_TILE - 1:Q_TILE - 1 + width]
    j = jnp.arange(width, dtype=jnp.int32)[None, :]
    before = rb[nb - 1][:, None, None]
    after = rb[2 * nb - 1][:, None, None]
    near = band - jnp.where(j[None] < own, before, after)
    key_pos = jnp.arange(seq, dtype=jnp.int32)[None, :, None]
    tile_start = (jnp.arange(n_q, dtype=jnp.int32) * Q_TILE)[None, None, :]
    far = jnp.where(key_pos < tile_start, before, after)
    hi = far.astype(BF16)
    lo = (far - hi.astype(F32)).astype(BF16)
    pad = jnp.zeros((ATTN_HEADS, seq, HEAD_DIM - 2 * n_q), BF16)
    return jnp.concatenate([hi, lo, pad], axis=-1), near


N_ATTN_UNITS = 2 * Q_TILES_PER_STEP


def _attn_body(lam_ref, q_ref, k_ref, v_ref, far_ref, near_ref, norm_ref, o_ref,
               kx_ref, vx_ref, qx_ref, m_ref, *unit_refs):
    step = pl.program_id(2)
    seq = k_ref.shape[1]
    n_q = seq // Q_TILE
    n_kc = seq // KEY_CHUNK
    s_refs = unit_refs[:N_ATTN_UNITS]
    p_refs = unit_refs[N_ATTN_UNITS:]

    @pl.when(step == 0)
    def _():
        kx_ref[:, :HEAD_DIM] = k_ref[0]
        kx_ref[:, HEAD_DIM:] = far_ref[0]
        vx_ref[:, :HEAD_DIM] = v_ref[0]
        vx_ref[:, HEAD_DIM:] = jnp.ones((seq, HEAD_DIM), BF16)

    lane = lax.broadcasted_iota(jnp.int32, (Q_TILE, HEAD_DIM), 1)
    zero = jnp.zeros((Q_TILE, HEAD_DIM), BF16)
    for t in range(Q_TILES_PER_STEP):
        q = q_ref[0, t * Q_TILE:(t + 1) * Q_TILE, :]
        tile = step * Q_TILES_PER_STEP + t
        tile_onehot = jnp.where((lane == tile) | (lane == n_q + tile), 1.0, 0.0).astype(BF16)
        qx_ref[2 * t, :, :HEAD_DIM] = jnp.where(lane < ATTN_HALF_DIM, q, zero)
        qx_ref[2 * t + 1, :, :HEAD_DIM] = jnp.where(lane >= ATTN_HALF_DIM, q, zero)
        qx_ref[2 * t, :, HEAD_DIM:] = tile_onehot
        qx_ref[2 * t + 1, :, HEAD_DIM:] = tile_onehot
    m_ref[...] = jnp.full(m_ref.shape, -jnp.inf, F32)

    chunks_per_iter = min(CHUNKS_PER_ITER, n_kc)
    assert n_kc % chunks_per_iter == 0

    def score_chunks(it, carry):
        for j in range(chunks_per_iter):
            kc = it * chunks_per_iter + j
            cols = pl.ds(pl.multiple_of(kc * KEY_CHUNK, KEY_CHUNK), KEY_CHUNK)
            for t in range(Q_TILES_PER_STEP):
                rel = jnp.clip(kc - (step * Q_TILES_PER_STEP + t), -NEAR_REACH - 1, NEAR_REACH + 1)
                near = near_ref[0, :, pl.ds(pl.multiple_of((rel + NEAR_REACH + 1) * KEY_CHUNK, KEY_CHUNK),
                                            KEY_CHUNK)]
                for u in (2 * t, 2 * t + 1):
                    s = lax.dot_general(qx_ref[u], kx_ref[cols, :], NT_DIMS,
                                        preferred_element_type=F32) + near
                    s_refs[u][:, cols] = s
                    m_ref[u] = jnp.maximum(m_ref[u], jnp.maximum(s[:, :LANES], s[:, LANES:]))
        return carry
    lax.fori_loop(0, n_kc // chunks_per_iter, score_chunks, 0)

    outs = []
    for u in range(N_ATTN_UNITS):
        m = jnp.max(m_ref[u], axis=-1, keepdims=True)
        for kc in range(n_kc):
            cols = slice(kc * KEY_CHUNK, (kc + 1) * KEY_CHUNK)
            p_refs[u][:, cols] = jnp.exp2((s_refs[u][:, cols] - m).astype(BF16))
        acc = _dot(p_refs[u][...], vx_ref[...])
        outs.append(acc[:, :HEAD_DIM] / acc[:, HEAD_DIM:])
    for t in range(Q_TILES_PER_STEP):
        o = outs[2 * t] - lam_ref[0] * outs[2 * t + 1]
        o_ref[0, t * Q_TILE:(t + 1) * Q_TILE, :] = (
            _rms(o, norm_ref[...]) * (1.0 - LAMBDA_INIT)).astype(o_ref.dtype)


def _attention(q, k, v, far, near, lam, head_norm):
    b, s, _ = q.shape
    rows = Q_TILES_PER_STEP * Q_TILE
    smem = pl.BlockSpec(memory_space=pltpu.SMEM)
    return pl.pallas_call(
        _attn_body,
        grid=(b, ATTN_HEADS, s // rows),
        in_specs=[smem,
                  pl.BlockSpec((1, rows, HEAD_DIM), lambda bi, h, i: (bi, i, h)),
                  pl.BlockSpec((1, s, HEAD_DIM), lambda bi, h, i: (bi, 0, h)),
                  pl.BlockSpec((1, s, HEAD_DIM), lambda bi, h, i: (bi, 0, h)),
                  pl.BlockSpec((1, s, HEAD_DIM), lambda bi, h, i: (h, 0, 0)),
                  pl.BlockSpec((1,) + near.shape[1:], lambda bi, h, i: (h, 0, 0)),
                  _resident(head_norm.shape)],
        out_specs=pl.BlockSpec((1, rows, HEAD_DIM), lambda bi, h, i: (bi, i, h)),
        out_shape=jax.ShapeDtypeStruct((b, s, ATTN_WIDTH), BF16),
        scratch_shapes=[pltpu.VMEM((s, 2 * HEAD_DIM), BF16), pltpu.VMEM((s, 2 * HEAD_DIM), BF16),
                        pltpu.VMEM((N_ATTN_UNITS, Q_TILE, 2 * HEAD_DIM), BF16),
                        pltpu.VMEM((N_ATTN_UNITS, Q_TILE, LANES), F32)]
                       + [pltpu.VMEM((Q_TILE, s), F32)] * N_ATTN_UNITS
                       + [pltpu.VMEM((Q_TILE, s), BF16)] * N_ATTN_UNITS,
        compiler_params=_params("parallel", "parallel", "arbitrary"),
        name="attn",
    )(lam, q, k, v, far, near, head_norm)


def _chunk_triangles():
    t = np.arange(RNN_GROUP)[:, None]
    s = np.arange(RNN_GROUP)[None, :]
    same = (t // CHUNK) == (s // CHUNK)
    return np.stack([same & (s <= t), same & (s >= t)]).astype(np.float32)


def _hgrn2_body(q_ref, v_ref, gate_ref, kf_ref, kb_ref, gf_ref, gb_ref, tri_ref, norm_ref, o_ref,
                acc_ref, qd_ref, qm_ref, km_ref, kd_ref, u_ref, dl_ref, sp_ref):
    seq = q_ref.shape[1]
    n_groups = seq // RNN_GROUP
    n_chunks = seq // CHUNK
    per_group = RNN_GROUP // CHUNK
    k_refs = (kf_ref, kb_ref)
    g_refs = (gf_ref, gb_ref)
    mid_row = (CHUNK // 2 - 1, CHUNK // 2)
    last_row = (CHUNK - 1, 0)

    def chunked(t):
        return t.reshape(per_group, CHUNK, HEAD_DIM)

    def flat(t):
        return t.reshape(RNN_GROUP, HEAD_DIM)

    def group_rows(gi):
        return pl.ds(pl.multiple_of(gi * RNN_GROUP, RNN_GROUP), RNN_GROUP)

    def decays(gi):
        rows = group_rows(gi)
        q = chunked(q_ref[0, rows, :].astype(F32))
        for d in range(2):
            log_f = g_refs[d][0, rows, :]
            hi = log_f.astype(BF16)
            lo = (log_f - hi.astype(F32)).astype(BF16)
            both = _dot(tri_ref[d], jnp.concatenate([hi, lo], axis=1))
            cum = chunked(both[:, :HEAD_DIM] + both[:, HEAD_DIM:])
            cum_mid = cum[:, mid_row[d]:mid_row[d] + 1, :]
            cum_last = cum[:, last_row[d]:last_row[d] + 1, :]
            k = chunked(k_refs[d][0, rows, :].astype(F32))
            q_mid = q * jnp.exp(cum - cum_mid)
            k_mid = k * jnp.exp(cum_mid - cum)
            lanes = slice(d * HEAD_DIM, (d + 1) * HEAD_DIM)
            qm_ref[d, rows, :] = flat(q_mid).astype(BF16)
            km_ref[d, rows, :] = flat(k_mid).astype(BF16)
            qd_ref[rows, lanes] = flat(q_mid * jnp.exp(cum_mid)).astype(BF16)
            kd_ref[rows, lanes] = flat(k_mid * jnp.exp(cum_last - cum_mid)).astype(BF16)
            decay = jnp.exp(cum_last)
            for j in range(per_group):
                dl_ref[d, gi * per_group + j] = decay[j]

    def products(gi):
        rows = group_rows(gi)
        v = v_ref[0, rows, :]
        a = None
        for d in range(2):
            a_d = lax.dot_general(qm_ref[d, rows, :], km_ref[d, rows, :], NT_DIMS,
                                  preferred_element_type=F32).astype(BF16)
            a_d = jnp.where(tri_ref[d] > 0, a_d, jnp.zeros_like(a_d))
            a = a_d if a is None else a + a_d
        acc_ref[rows, :] = _dot(a, v)
        k_dec = kd_ref[rows, :]
        for j in range(per_group):
            chunk = slice(j * CHUNK, (j + 1) * CHUNK)
            u = lax.dot_general(v[chunk], k_dec[chunk], TN_DIMS, preferred_element_type=F32)
            for d in range(2):
                u_ref[d, gi * per_group + j] = u[:, d * HEAD_DIM:(d + 1) * HEAD_DIM]

    decays(0)

    def chunk_local(gi, carry):
        products(gi - 1)
        decays(gi)
        return carry
    lax.fori_loop(1, n_groups, chunk_local, 0, unroll=3)
    products(n_groups - 1)

    def scan(i, states):
        new = []
        for d, state in enumerate(states):
            ci = i if d == 0 else n_chunks - 1 - i
            sp_ref[ci, :, d * HEAD_DIM:(d + 1) * HEAD_DIM] = state.astype(BF16)
            new.append(state * dl_ref[d, ci] + u_ref[d, ci])
        return tuple(new)
    zero = jnp.zeros((HEAD_DIM, HEAD_DIM), F32)
    lax.fori_loop(0, n_chunks, scan, (zero, zero))

    def outputs(gi, carry):
        rows = group_rows(gi)
        entering = [lax.dot_general(qd_ref[pl.ds(pl.multiple_of(gi * RNN_GROUP + j * CHUNK, CHUNK), CHUNK), :],
                                    sp_ref[gi * per_group + j], NT_DIMS,
                                    preferred_element_type=F32) for j in range(per_group)]
        total = acc_ref[rows, :] + jnp.concatenate(entering, axis=0)
        gate = gate_ref[0, rows, :].astype(F32)
        o_ref[0, rows, :] = (_rms(total, norm_ref[...]) * gate).astype(o_ref.dtype)
        return carry
    lax.fori_loop(0, n_groups, outputs, 0, unroll=4)


def _hgrn2(q, v, gate, k_fw, k_bw, g_fw, g_bw, head_norm):
    b, s, _ = q.shape
    n_chunks = s // CHUNK
    tri = jnp.asarray(_chunk_triangles(), BF16)
    seq_block = pl.BlockSpec((1, s, HEAD_DIM), lambda bi, h: (bi, 0, h))
    return pl.pallas_call(
        _hgrn2_body,
        grid=(b, RNN_HEADS),
        in_specs=[seq_block] * 7 + [_resident(tri.shape), _resident(head_norm.shape)],
        out_specs=seq_block,
        out_shape=jax.ShapeDtypeStruct((b, s, RNN_WIDTH), BF16),
        scratch_shapes=[pltpu.VMEM((s, HEAD_DIM), F32),
                        pltpu.VMEM((s, 2 * HEAD_DIM), BF16),
                        pltpu.VMEM((2, s, HEAD_DIM), BF16),
                        pltpu.VMEM((2, s, HEAD_DIM), BF16),
                        pltpu.VMEM((s, 2 * HEAD_DIM), BF16),
                        pltpu.VMEM((2, n_chunks, HEAD_DIM, HEAD_DIM), F32),
                        pltpu.VMEM((2, n_chunks, 1, HEAD_DIM), F32),
                        pltpu.VMEM((n_chunks, HEAD_DIM, 2 * HEAD_DIM), BF16)],
        compiler_params=_params("parallel", "parallel"),
        name="hgrn2",
    )(q, v, gate, k_fw, k_bw, g_fw, g_bw, tri, head_norm)


def kernel(x, rel_bias, lb_logits, ffn1_pre_norm, ffn1_w_in, ffn1_w_out, ffn1_post_norm, mix_pre_norm, w_mix_in, lambda_q1, lambda_k1, lambda_q2, lambda_k2, attn_head_norm, rnn_head_norm, w_mix_out, mix_post_norm, ffn2_pre_norm, ffn2_w_in, ffn2_w_out, ffn2_post_norm):
    b, s, d = x.shape
    layer = 0
    row = lambda t: t[layer].reshape(1, -1).astype(F32)

    h1 = _ffn(x.reshape(b * s, d), row(ffn1_pre_norm), ffn1_w_in[layer].astype(BF16),
              ffn1_w_out[layer].astype(BF16), row(ffn1_post_norm))

    lb = jnp.cumsum(jax.nn.softmax(lb_logits.astype(F32), axis=1), axis=1)[:, layer]
    q_a, k_a, v_a, q_r, i_r, g_r, k_fw, k_bw, g_fw, g_bw = _mixproj(
        h1, row(mix_pre_norm), w_mix_in[layer].astype(BF16), lb.reshape(2, 1, RNN_WIDTH))
    seq = lambda t: t.reshape(b, s, t.shape[-1])

    lam = (jnp.exp(jnp.sum(lambda_q1[layer] * lambda_k1[layer]).astype(F32))
           - jnp.exp(jnp.sum(lambda_q2[layer] * lambda_k2[layer]).astype(F32))
           + LAMBDA_INIT).reshape(1)
    far, near = _bias_tables(rel_bias, s)
    o_a = _attention(seq(q_a), seq(k_a), seq(v_a), far, near, lam, row(attn_head_norm))

    o_r = _hgrn2(seq(q_r), seq(i_r), seq(g_r), seq(k_fw), seq(k_bw), seq(g_fw), seq(g_bw),
                 row(rnn_head_norm))

    y = _mix_ffn(h1, o_a.reshape(b * s, ATTN_WIDTH), o_r.reshape(b * s, RNN_WIDTH),
                 w_mix_out[layer].astype(BF16), row(mix_post_norm), row(ffn2_pre_norm),
                 ffn2_w_in[layer].astype(BF16), ffn2_w_out[layer].astype(BF16),
                 row(ffn2_post_norm))
    return y.reshape(b, s, d).astype(x.dtype)
```

```python
import functools
import math

import numpy as np
import jax
import jax.numpy as jnp
from jax import lax
from jax.experimental import pallas as pl
from jax.experimental.pallas import tpu as pltpu

F32 = jnp.float32
BF16 = jnp.bfloat16

D_MODEL = 1024
ATTN_WIDTH = 512
RNN_WIDTH = 512
ATTN_HEADS = 4
ATTN_HALF_DIM = 64
HEAD_DIM = 128
RNN_HEADS = 4
D_FF = 2816
N_BUCKETS = 32
MAX_DISTANCE = 128
CHUNK = 64
EPS = 1e-6
LAMBDA_INIT = 0.8 - 0.6 * math.exp(-0.3 * 0)

VMEM_LIMIT_BYTES = 56 * 1024 * 1024
LANES = 128

TOKEN_TILE = 512
MIXPROJ_TILE = 1024
FF_CHUNK = 256
Q_TILE = 256
Q_TILES_PER_STEP = 2
KEY_CHUNK = 256
CHUNKS_PER_ITER = 8
NEAR_REACH = 1
LOG2E = math.log2(math.e)
RNN_GROUP = 256

NT_DIMS = (((1,), (1,)), ((), ()))
TN_DIMS = (((0,), (0,)), ((), ()))


def _rms(x, w):
    return x * lax.rsqrt(jnp.mean(x * x, axis=-1, keepdims=True) + EPS) * w


def _dot(a, b):
    return jnp.dot(a, b, preferred_element_type=F32)


def _sigmoid(x):
    return 0.5 * jnp.tanh(0.5 * x) + 0.5


def _silu(x):
    return x * _sigmoid(x)


def _resident(shape):
    return pl.BlockSpec(shape, lambda *_: (0,) * len(shape), pipeline_mode=pl.Buffered(1))


def _params(*semantics):
    return pltpu.CompilerParams(dimension_semantics=semantics, vmem_limit_bytes=VMEM_LIMIT_BYTES)


def _swiglu_half_step(h, pre_ref, w_in_ref, w_out_ref, post_ref, acc_ref):
    u = _rms(h, pre_ref[...]).astype(BF16)
    for c in range(D_FF // FF_CHUNK):
        lo = c * FF_CHUNK
        gate = _dot(u, w_in_ref[:, lo:lo + FF_CHUNK].astype(BF16))
        up = _dot(u, w_in_ref[:, D_FF + lo:D_FF + lo + FF_CHUNK].astype(BF16))
        act = (_silu(gate) * up).astype(BF16)
        part = _dot(act, w_out_ref[lo:lo + FF_CHUNK, :].astype(BF16))
        if c == 0:
            acc_ref[...] = part
        else:
            acc_ref[...] += part
    return h + 0.5 * _rms(acc_ref[...], post_ref[...])


def _ffn_body(x_ref, pre_ref, w_in_ref, w_out_ref, post_ref, o_ref, acc_ref):
    o_ref[...] = _swiglu_half_step(x_ref[...], pre_ref, w_in_ref, w_out_ref, post_ref, acc_ref)


def _ffn(x, pre, w_in, w_out, post):
    n, d = x.shape
    tile = pl.BlockSpec((TOKEN_TILE, d), lambda i: (i, 0))
    return pl.pallas_call(
        _ffn_body,
        grid=(n // TOKEN_TILE,),
        in_specs=[tile, _resident(pre.shape), _resident(w_in.shape), _resident(w_out.shape),
                  _resident(post.shape)],
        out_specs=tile,
        out_shape=jax.ShapeDtypeStruct((n, d), F32),
        scratch_shapes=[pltpu.VMEM((TOKEN_TILE, d), F32)],
        compiler_params=_params("parallel"),
        name="ffn",
    )(x, pre, w_in, w_out, post)


def _mix_ffn_body(h_ref, oa_ref, or_ref, w_mix_ref, mix_post_ref, pre_ref, w_in_ref, w_out_ref,
                  post_ref, o_ref, acc_ref):
    mixed = (_dot(oa_ref[...], w_mix_ref[:ATTN_WIDTH, :].astype(BF16))
             + _dot(or_ref[...], w_mix_ref[ATTN_WIDTH:, :].astype(BF16)))
    h = h_ref[...] + _rms(mixed, mix_post_ref[...])
    o_ref[...] = _swiglu_half_step(h, pre_ref, w_in_ref, w_out_ref, post_ref, acc_ref)


def _mix_ffn(h, o_a, o_r, w_mix, mix_post, pre, w_in, w_out, post):
    n, d = h.shape
    tile = pl.BlockSpec((TOKEN_TILE, d), lambda i: (i, 0))
    half = pl.BlockSpec((TOKEN_TILE, ATTN_WIDTH), lambda i: (i, 0))
    return pl.pallas_call(
        _mix_ffn_body,
        grid=(n // TOKEN_TILE,),
        in_specs=[tile, half, half, _resident(w_mix.shape), _resident(mix_post.shape),
                  _resident(pre.shape), _resident(w_in.shape), _resident(w_out.shape),
                  _resident(post.shape)],
        out_specs=tile,
        out_shape=jax.ShapeDtypeStruct((n, d), F32),
        scratch_shapes=[pltpu.VMEM((TOKEN_TILE, d), F32)],
        compiler_params=_params("parallel"),
        name="mix_ffn",
    )(h, o_a, o_r, w_mix, mix_post, pre, w_in, w_out, post)


PROJ_WIDTH = 512
MIXPROJ_DTYPES = [BF16] * 8 + [F32] * 2


def _mixproj_body(h_ref, norm_ref, w_ref, lb_ref, qa_ref, ka_ref, va_ref, qr_ref, ir_ref, gr_ref,
                  kf_ref, kb_ref, gf_ref, gb_ref):
    u = _rms(h_ref[...], norm_ref[...]).astype(BF16)

    def proj(j):
        return _dot(u, w_ref[:, j * PROJ_WIDTH:(j + 1) * PROJ_WIDTH].astype(BF16))

    for direction, (k_ref, g_ref) in enumerate(((kf_ref, gf_ref), (kb_ref, gb_ref))):
        lb = lb_ref[direction]
        sig = _sigmoid(proj(5 + direction))
        k_ref[...] = ((1.0 - lb) * (1.0 - sig)).astype(BF16)
        g_ref[...] = jnp.log(lb + (1.0 - lb) * sig)
    qr_ref[...] = _silu(proj(3)).astype(BF16)
    gr_ref[...] = _silu(proj(7)).astype(BF16)
    qa_ref[...] = (proj(0) * (ATTN_HALF_DIM ** -0.5 * LOG2E)).astype(BF16)
    ka_ref[...] = proj(1).astype(BF16)
    va_ref[...] = proj(2).astype(BF16)
    ir_ref[...] = proj(4).astype(BF16)


def _mixproj(h, norm, w, lb):
    n, d = h.shape
    tile = pl.BlockSpec((MIXPROJ_TILE, d), lambda i: (i, 0))
    out_tile = pl.BlockSpec((MIXPROJ_TILE, PROJ_WIDTH), lambda i: (i, 0))
    return pl.pallas_call(
        _mixproj_body,
        grid=(n // MIXPROJ_TILE,),
        in_specs=[tile, _resident(norm.shape), _resident(w.shape), _resident(lb.shape)],
        out_specs=[out_tile] * len(MIXPROJ_DTYPES),
        out_shape=[jax.ShapeDtypeStruct((n, PROJ_WIDTH), dt) for dt in MIXPROJ_DTYPES],
        compiler_params=_params("parallel"),
        name="mixproj",
    )(h, norm, w, lb)


def _rel_bucket(rel):
    nb = N_BUCKETS // 2
    max_exact = nb // 2
    side = jnp.where(rel > 0, nb, 0)
    n = jnp.abs(rel)
    nf = jnp.maximum(n, 1).astype(jnp.float32)
    large = max_exact + (jnp.log(nf / max_exact) / math.log(MAX_DISTANCE / max_exact)
                         * (nb - max_exact)).astype(jnp.int32)
    large = jnp.minimum(large, nb - 1)
    return side + jnp.where(n < max_exact, n, large)


def _bias_tables(rel_bias, seq):
    nb = N_BUCKETS // 2
    n_q = seq // Q_TILE
    assert 2 * n_q <= HEAD_DIM and KEY_CHUNK == Q_TILE and NEAR_REACH * KEY_CHUNK >= MAX_DISTANCE
    rb = rel_bias.astype(F32) * LOG2E
    width = (2 * NEAR_REACH + 3) * KEY_CHUNK
    own = (NEAR_REACH + 1) * KEY_CHUNK
    i = jnp.arange(Q_TILE, dtype=jnp.int32)[:, None]
    j = jnp.arange(width, dtype=jnp.int32)[None, :]
    bucket = _rel_bucket(j - own - i)
    band = jnp.zeros((ATTN_HEADS, Q_TILE, width), F32)
    for n in range(N_BUCKETS):
        band = jnp.where(bucket[None] == n, rb[n][:, None, None], band)
    before = rb[nb - 1][:, None, None]
    after = rb[2 * nb - 1][:, None, None]
    near = band - jnp.where(j[None] < own, before, after)
    key_pos = jnp.arange(seq, dtype=jnp.int32)[None, :, None]
    tile_start = (jnp.arange(n_q, dtype=jnp.int32) * Q_TILE)[None, None, :]
    far = jnp.where(key_pos < tile_start, before, after)
    hi = far.astype(BF16)
    lo = (far - hi.astype(F32)).astype(BF16)
    pad = jnp.zeros((ATTN_HEADS, seq, HEAD_DIM - 2 * n_q), BF16)
    return jnp.concatenate([hi, lo, pad], axis=-1), near


N_ATTN_UNITS = 2 * Q_TILES_PER_STEP


def _attn_body(lam_ref, q_ref, k_ref, v_ref, far_ref, near_ref, norm_ref, o_ref,
               kx_ref, vx_ref, qx_ref, m_ref, *unit_refs):
    step = pl.program_id(2)
    seq = k_ref.shape[1]
    n_q = seq // Q_TILE
    n_kc = seq // KEY_CHUNK
    s_refs = unit_refs[:N_ATTN_UNITS]
    p_refs = unit_refs[N_ATTN_UNITS:]

    @pl.when(step == 0)
    def _():
        kx_ref[:, :HEAD_DIM] = k_ref[0]
        kx_ref[:, HEAD_DIM:] = far_ref[0]
        vx_ref[:, :HEAD_DIM] = v_ref[0]
        vx_ref[:, HEAD_DIM:] = jnp.ones((seq, HEAD_DIM), BF16)

    lane = lax.broadcasted_iota(jnp.int32, (Q_TILE, HEAD_DIM), 1)
    zero = jnp.zeros((Q_TILE, HEAD_DIM), BF16)
    for t in range(Q_TILES_PER_STEP):
        q = q_ref[0, t * Q_TILE:(t + 1) * Q_TILE, :]
        tile = step * Q_TILES_PER_STEP + t
        tile_onehot = jnp.where((lane == tile) | (lane == n_q + tile), 1.0, 0.0).astype(BF16)
        qx_ref[2 * t, :, :HEAD_DIM] = jnp.where(lane < ATTN_HALF_DIM, q, zero)
        qx_ref[2 * t + 1, :, :HEAD_DIM] = jnp.where(lane >= ATTN_HALF_DIM, q, zero)
        qx_ref[2 * t, :, HEAD_DIM:] = tile_onehot
        qx_ref[2 * t + 1, :, HEAD_DIM:] = tile_onehot
    m_ref[...] = jnp.full(m_ref.shape, -jnp.inf, F32)

    chunks_per_iter = min(CHUNKS_PER_ITER, n_kc)
    assert n_kc % chunks_per_iter == 0

    def score_chunks(it, carry):
        for j in range(chunks_per_iter):
            kc = it * chunks_per_iter + j
            cols = pl.ds(pl.multiple_of(kc * KEY_CHUNK, KEY_CHUNK), KEY_CHUNK)
            for t in range(Q_TILES_PER_STEP):
                rel = jnp.clip(kc - (step * Q_TILES_PER_STEP + t), -NEAR_REACH - 1, NEAR_REACH + 1)
                near = near_ref[0, :, pl.ds(pl.multiple_of((rel + NEAR_REACH + 1) * KEY_CHUNK, KEY_CHUNK),
                                            KEY_CHUNK)]
                for u in (2 * t, 2 * t + 1):
                    s = lax.dot_general(qx_ref[u], kx_ref[cols, :], NT_DIMS,
                                        preferred_element_type=F32) + near
                    s_refs[u][:, cols] = s
                    m_ref[u] = jnp.maximum(m_ref[u], jnp.maximum(s[:, :LANES], s[:, LANES:]))
        return carry
    lax.fori_loop(0, n_kc // chunks_per_iter, score_chunks, 0)

    outs = []
    for u in range(N_ATTN_UNITS):
        m = jnp.max(m_ref[u], axis=-1, keepdims=True)
        for kc in range(n_kc):
            cols = slice(kc * KEY_CHUNK, (kc + 1) * KEY_CHUNK)
            p_refs[u][:, cols] = jnp.exp2((s_refs[u][:, cols] - m).astype(BF16))
        acc = _dot(p_refs[u][...], vx_ref[...])
        outs.append(acc[:, :HEAD_DIM] / acc[:, HEAD_DIM:])
    for t in range(Q_TILES_PER_STEP):
        o = outs[2 * t] - lam_ref[0] * outs[2 * t + 1]
        o_ref[0, t * Q_TILE:(t + 1) * Q_TILE, :] = (
            _rms(o, norm_ref[...]) * (1.0 - LAMBDA_INIT)).astype(o_ref.dtype)


def _attention(q, k, v, far, near, lam, head_norm):
    b, s, _ = q.shape
    rows = Q_TILES_PER_STEP * Q_TILE
    smem = pl.BlockSpec(memory_space=pltpu.SMEM)
    return pl.pallas_call(
        _attn_body,
        grid=(b, ATTN_HEADS, s // rows),
        in_specs=[smem,
                  pl.BlockSpec((1, rows, HEAD_DIM), lambda bi, h, i: (bi, i, h)),
                  pl.BlockSpec((1, s, HEAD_DIM), lambda bi, h, i: (bi, 0, h)),
                  pl.BlockSpec((1, s, HEAD_DIM), lambda bi, h, i: (bi, 0, h)),
                  pl.BlockSpec((1, s, HEAD_DIM), lambda bi, h, i: (h, 0, 0)),
                  pl.BlockSpec((1,) + near.shape[1:], lambda bi, h, i: (h, 0, 0)),
                  _resident(head_norm.shape)],
        out_specs=pl.BlockSpec((1, rows, HEAD_DIM), lambda bi, h, i: (bi, i, h)),
        out_shape=jax.ShapeDtypeStruct((b, s, ATTN_WIDTH), BF16),
        scratch_shapes=[pltpu.VMEM((s, 2 * HEAD_DIM), BF16), pltpu.VMEM((s, 2 * HEAD_DIM), BF16),
                        pltpu.VMEM((N_ATTN_UNITS, Q_TILE, 2 * HEAD_DIM), BF16),
                        pltpu.VMEM((N_ATTN_UNITS, Q_TILE, LANES), F32)]
                       + [pltpu.VMEM((Q_TILE, s), F32)] * N_ATTN_UNITS
                       + [pltpu.VMEM((Q_TILE, s), BF16)] * N_ATTN_UNITS,
        compiler_params=_params("parallel", "parallel", "arbitrary"),
        name="attn",
    )(lam, q, k, v, far, near, head_norm)


def _chunk_triangles():
    t = np.arange(RNN_GROUP)[:, None]
    s = np.arange(RNN_GROUP)[None, :]
    same = (t // CHUNK) == (s // CHUNK)
    return np.stack([same & (s <= t), same & (s >= t)]).astype(np.float32)


def _hgrn2_body(q_ref, v_ref, gate_ref, kf_ref, kb_ref, gf_ref, gb_ref, tri_ref, norm_ref, o_ref,
                acc_ref, qd_ref, qm_ref, km_ref, kd_ref, u_ref, dl_ref, sp_ref):
    seq = q_ref.shape[1]
    n_groups = seq // RNN_GROUP
    n_chunks = seq // CHUNK
    per_group = RNN_GROUP // CHUNK
    k_refs = (kf_ref, kb_ref)
    g_refs = (gf_ref, gb_ref)
    mid_row = (CHUNK // 2 - 1, CHUNK // 2)
    last_row = (CHUNK - 1, 0)

    def chunked(t):
        return t.reshape(per_group, CHUNK, HEAD_DIM)

    def flat(t):
        return t.reshape(RNN_GROUP, HEAD_DIM)

    def group_rows(gi):
        return pl.ds(pl.multiple_of(gi * RNN_GROUP, RNN_GROUP), RNN_GROUP)

    def decays(gi):
        rows = group_rows(gi)
        q = chunked(q_ref[0, rows, :].astype(F32))
        for d in range(2):
            log_f = g_refs[d][0, rows, :]
            hi = log_f.astype(BF16)
            lo = (log_f - hi.astype(F32)).astype(BF16)
            both = _dot(tri_ref[d], jnp.concatenate([hi, lo], axis=1))
            cum = chunked(both[:, :HEAD_DIM] + both[:, HEAD_DIM:])
            cum_mid = cum[:, mid_row[d]:mid_row[d] + 1, :]
            cum_last = cum[:, last_row[d]:last_row[d] + 1, :]
            k = chunked(k_refs[d][0, rows, :].astype(F32))
            q_mid = q * jnp.exp(cum - cum_mid)
            k_mid = k * jnp.exp(cum_mid - cum)
            lanes = slice(d * HEAD_DIM, (d + 1) * HEAD_DIM)
            qm_ref[d, rows, :] = flat(q_mid).astype(BF16)
            km_ref[d, rows, :] = flat(k_mid).astype(BF16)
            qd_ref[rows, lanes] = flat(q_mid * jnp.exp(cum_mid)).astype(BF16)
            kd_ref[rows, lanes] = flat(k_mid * jnp.exp(cum_last - cum_mid)).astype(BF16)
            decay = jnp.exp(cum_last)
            for j in range(per_group):
                dl_ref[d, gi * per_group + j] = decay[j]

    def products(gi):
        rows = group_rows(gi)
        v = v_ref[0, rows, :]
        a = None
        for d in range(2):
            a_d = lax.dot_general(qm_ref[d, rows, :], km_ref[d, rows, :], NT_DIMS,
                                  preferred_element_type=F32).astype(BF16)
            a_d = jnp.where(tri_ref[d] > 0, a_d, jnp.zeros_like(a_d))
            a = a_d if a is None else a + a_d
        acc_ref[rows, :] = _dot(a, v)
        k_dec = kd_ref[rows, :]
        for j in range(per_group):
            chunk = slice(j * CHUNK, (j + 1) * CHUNK)
            u = lax.dot_general(v[chunk], k_dec[chunk], TN_DIMS, preferred_element_type=F32)
            for d in range(2):
                u_ref[d, gi * per_group + j] = u[:, d * HEAD_DIM:(d + 1) * HEAD_DIM]

    decays(0)

    def chunk_local(gi, carry):
        products(gi - 1)
        decays(gi)
        return carry
    lax.fori_loop(1, n_groups, chunk_local, 0, unroll=3)
    products(n_groups - 1)

    def scan(i, states):
        new = []
        for d, state in enumerate(states):
            ci = i if d == 0 else n_chunks - 1 - i
            sp_ref[ci, :, d * HEAD_DIM:(d + 1) * HEAD_DIM] = state.astype(BF16)
            new.append(state * dl_ref[d, ci] + u_ref[d, ci])
        return tuple(new)
    zero = jnp.zeros((HEAD_DIM, HEAD_DIM), F32)
    lax.fori_loop(0, n_chunks, scan, (zero, zero))

    def outputs(gi, carry):
        rows = group_rows(gi)
        entering = [lax.dot_general(qd_ref[pl.ds(pl.multiple_of(gi * RNN_GROUP + j * CHUNK, CHUNK), CHUNK), :],
                                    sp_ref[gi * per_group + j], NT_DIMS,
                                    preferred_element_type=F32) for j in range(per_group)]
        total = acc_ref[rows, :] + jnp.concatenate(entering, axis=0)
        gate = gate_ref[0, rows, :].astype(F32)
        o_ref[0, rows, :] = (_rms(total, norm_ref[...]) * gate).astype(o_ref.dtype)
        return carry
    lax.fori_loop(0, n_groups, outputs, 0, unroll=4)


def _hgrn2(q, v, gate, k_fw, k_bw, g_fw, g_bw, head_norm):
    b, s, _ = q.shape
    n_chunks = s // CHUNK
    tri = jnp.asarray(_chunk_triangles(), BF16)
    seq_block = pl.BlockSpec((1, s, HEAD_DIM), lambda bi, h: (bi, 0, h))
    return pl.pallas_call(
        _hgrn2_body,
        grid=(b, RNN_HEADS),
        in_specs=[seq_block] * 7 + [_resident(tri.shape), _resident(head_norm.shape)],
        out_specs=seq_block,
        out_shape=jax.ShapeDtypeStruct((b, s, RNN_WIDTH), BF16),
        scratch_shapes=[pltpu.VMEM((s, HEAD_DIM), F32),
                        pltpu.VMEM((s, 2 * HEAD_DIM), BF16),
                        pltpu.VMEM((2, s, HEAD_DIM), BF16),
                        pltpu.VMEM((2, s, HEAD_DIM), BF16),
                        pltpu.VMEM((s, 2 * HEAD_DIM), BF16),
                        pltpu.VMEM((2, n_chunks, HEAD_DIM, HEAD_DIM), F32),
                        pltpu.VMEM((2, n_chunks, 1, HEAD_DIM), F32),
                        pltpu.VMEM((n_chunks, HEAD_DIM, 2 * HEAD_DIM), BF16)],
        compiler_params=_params("parallel", "parallel"),
        name="hgrn2",
    )(q, v, gate, k_fw, k_bw, g_fw, g_bw, tri, head_norm)


def kernel(x, rel_bias, lb_logits, ffn1_pre_norm, ffn1_w_in, ffn1_w_out, ffn1_post_norm, mix_pre_norm, w_mix_in, lambda_q1, lambda_k1, lambda_q2, lambda_k2, attn_head_norm, rnn_head_norm, w_mix_out, mix_post_norm, ffn2_pre_norm, ffn2_w_in, ffn2_w_out, ffn2_post_norm):
    b, s, d = x.shape
    layer = 0
    row = lambda t: t[layer].reshape(1, -1).astype(F32)

    h1 = _ffn(x.reshape(b * s, d), row(ffn1_pre_norm), ffn1_w_in[layer], ffn1_w_out[layer],
              row(ffn1_post_norm))

    lb = jnp.cumsum(jax.nn.softmax(lb_logits.astype(F32), axis=1), axis=1)[:, layer]
    q_a, k_a, v_a, q_r, i_r, g_r, k_fw, k_bw, g_fw, g_bw = _mixproj(
        h1, row(mix_pre_norm), w_mix_in[layer], lb.reshape(2, 1, RNN_WIDTH))
    seq = lambda t: t.reshape(b, s, t.shape[-1])

    lam = (jnp.exp(jnp.sum(lambda_q1[layer] * lambda_k1[layer]).astype(F32))
           - jnp.exp(jnp.sum(lambda_q2[layer] * lambda_k2[layer]).astype(F32))
           + LAMBDA_INIT).reshape(1)
    far, near = _bias_tables(rel_bias, s)
    o_a = _attention(seq(q_a), seq(k_a), seq(v_a), far, near, lam, row(attn_head_norm))

    o_r = _hgrn2(seq(q_r), seq(i_r), seq(g_r), seq(k_fw), seq(k_bw), seq(g_fw), seq(g_bw),
                 row(rnn_head_norm))

    y = _mix_ffn(h1, o_a.reshape(b * s, ATTN_WIDTH), o_r.reshape(b * s, RNN_WIDTH),
                 w_mix_out[layer], row(mix_post_norm), row(ffn2_pre_norm),
                 ffn2_w_in[layer], ffn2_w_out[layer], row(ffn2_post_norm))
    return y.reshape(b, s, d).astype(x.dtype)
```

```python
import functools
import math

import numpy as np
import jax
import jax.numpy as jnp
from jax import lax
from jax.experimental import pallas as pl
from jax.experimental.pallas import tpu as pltpu

F32 = jnp.float32
BF16 = jnp.bfloat16

D_MODEL = 1024
ATTN_WIDTH = 512
RNN_WIDTH = 512
ATTN_HEADS = 4
ATTN_HALF_DIM = 64
HEAD_DIM = 128
RNN_HEADS = 4
D_FF = 2816
N_BUCKETS = 32
MAX_DISTANCE = 128
CHUNK = 64
EPS = 1e-6
LAMBDA_INIT = 0.8 - 0.6 * math.exp(-0.3 * 0)

VMEM_LIMIT_BYTES = 56 * 1024 * 1024
LANES = 128

TOKEN_TILE = 512
MIXPROJ_TILE = 1024
FF_CHUNK = 256
Q_TILE = 256
Q_TILES_PER_STEP = 2
KEY_CHUNK = 256
CHUNKS_PER_ITER = 8
NEAR_REACH = 1
LOG2E = math.log2(math.e)
RNN_GROUP = 256

NT_DIMS = (((1,), (1,)), ((), ()))
TN_DIMS = (((0,), (0,)), ((), ()))


def _rms(x, w):
    return x * lax.rsqrt(jnp.mean(x * x, axis=-1, keepdims=True) + EPS) * w


def _dot(a, b):
    return jnp.dot(a, b, preferred_element_type=F32)


def _silu(x):
    half = 0.5 * x
    return half + half * jnp.tanh(half)


def _resident(shape):
    return pl.BlockSpec(shape, lambda *_: (0,) * len(shape), pipeline_mode=pl.Buffered(1))


def _params(*semantics):
    return pltpu.CompilerParams(dimension_semantics=semantics, vmem_limit_bytes=VMEM_LIMIT_BYTES)


def _swiglu_half_step(h, pre_ref, w_in_ref, w_out_ref, post_ref, acc_ref):
    u = _rms(h, pre_ref[...]).astype(BF16)
    for c in range(D_FF // FF_CHUNK):
        lo = c * FF_CHUNK
        gate = _dot(u, w_in_ref[:, lo:lo + FF_CHUNK].astype(BF16))
        up = _dot(u, w_in_ref[:, D_FF + lo:D_FF + lo + FF_CHUNK].astype(BF16))
        act = (_silu(gate) * up).astype(BF16)
        part = _dot(act, w_out_ref[lo:lo + FF_CHUNK, :].astype(BF16))
        if c == 0:
            acc_ref[...] = part
        else:
            acc_ref[...] += part
    return h + 0.5 * _rms(acc_ref[...], post_ref[...])


def _ffn_body(x_ref, pre_ref, w_in_ref, w_out_ref, post_ref, o_ref, acc_ref):
    o_ref[...] = _swiglu_half_step(x_ref[...], pre_ref, w_in_ref, w_out_ref, post_ref, acc_ref)


def _ffn(x, pre, w_in, w_out, post):
    n, d = x.shape
    tile = pl.BlockSpec((TOKEN_TILE, d), lambda i: (i, 0))
    return pl.pallas_call(
        _ffn_body,
        grid=(n // TOKEN_TILE,),
        in_specs=[tile, _resident(pre.shape), _resident(w_in.shape), _resident(w_out.shape),
                  _resident(post.shape)],
        out_specs=tile,
        out_shape=jax.ShapeDtypeStruct((n, d), F32),
        scratch_shapes=[pltpu.VMEM((TOKEN_TILE, d), F32)],
        compiler_params=_params("parallel"),
        name="ffn",
    )(x, pre, w_in, w_out, post)


def _mix_ffn_body(h_ref, oa_ref, or_ref, w_mix_ref, mix_post_ref, pre_ref, w_in_ref, w_out_ref,
                  post_ref, o_ref, acc_ref):
    mixed = (_dot(oa_ref[...], w_mix_ref[:ATTN_WIDTH, :].astype(BF16))
             + _dot(or_ref[...], w_mix_ref[ATTN_WIDTH:, :].astype(BF16)))
    h = h_ref[...] + _rms(mixed, mix_post_ref[...])
    o_ref[...] = _swiglu_half_step(h, pre_ref, w_in_ref, w_out_ref, post_ref, acc_ref)


def _mix_ffn(h, o_a, o_r, w_mix, mix_post, pre, w_in, w_out, post):
    n, d = h.shape
    tile = pl.BlockSpec((TOKEN_TILE, d), lambda i: (i, 0))
    half = pl.BlockSpec((TOKEN_TILE, ATTN_WIDTH), lambda i: (i, 0))
    return pl.pallas_call(
        _mix_ffn_body,
        grid=(n // TOKEN_TILE,),
        in_specs=[tile, half, half, _resident(w_mix.shape), _resident(mix_post.shape),
                  _resident(pre.shape), _resident(w_in.shape), _resident(w_out.shape),
                  _resident(post.shape)],
        out_specs=tile,
        out_shape=jax.ShapeDtypeStruct((n, d), F32),
        scratch_shapes=[pltpu.VMEM((TOKEN_TILE, d), F32)],
        compiler_params=_params("parallel"),
        name="mix_ffn",
    )(h, o_a, o_r, w_mix, mix_post, pre, w_in, w_out, post)


PROJ_PART = 256
PROJ_WIDTH = 512
MIXPROJ_DTYPES = [BF16] * 8 + [F32] * 2


def _mixproj_body(h_ref, norm_ref, w_ref, lb_ref, qa_ref, ka_ref, va_ref, qr_ref, ir_ref, gr_ref,
                  kf_ref, kb_ref, gf_ref, gb_ref):
    u = _rms(h_ref[...], norm_ref[...]).astype(BF16)

    def silu_store(o_ref, cols, x):
        o_ref[:, cols] = _silu(x).astype(BF16)

    def gate_store(direction, k_ref, g_ref, cols, z):
        lb = lb_ref[direction][:, cols]
        c1 = 0.5 * (1.0 - lb)
        ct = c1 * jnp.tanh(0.5 * z)
        k_ref[:, cols] = (c1 - ct).astype(BF16)
        g_ref[:, cols] = jnp.log((lb + c1) + ct)

    def cast_store(o_ref, cols, x):
        o_ref[:, cols] = x.astype(BF16)

    def scaled_store(o_ref, cols, x):
        o_ref[:, cols] = (x * (ATTN_HALF_DIM ** -0.5 * LOG2E)).astype(BF16)

    plan = [(5, functools.partial(gate_store, 0, kf_ref, gf_ref)),
            (6, functools.partial(gate_store, 1, kb_ref, gb_ref)),
            (3, functools.partial(silu_store, qr_ref)),
            (7, functools.partial(silu_store, gr_ref)),
            (0, functools.partial(scaled_store, qa_ref)),
            (1, functools.partial(cast_store, ka_ref)),
            (2, functools.partial(cast_store, va_ref)),
            (4, functools.partial(cast_store, ir_ref))]
    for j, store in plan:
        for part in range(PROJ_WIDTH // PROJ_PART):
            cols = slice(part * PROJ_PART, (part + 1) * PROJ_PART)
            lo = j * PROJ_WIDTH + part * PROJ_PART
            store(cols, _dot(u, w_ref[:, lo:lo + PROJ_PART].astype(BF16)))


def _mixproj(h, norm, w, lb):
    n, d = h.shape
    tile = pl.BlockSpec((MIXPROJ_TILE, d), lambda i: (i, 0))
    out_tile = pl.BlockSpec((MIXPROJ_TILE, PROJ_WIDTH), lambda i: (i, 0))
    return pl.pallas_call(
        _mixproj_body,
        grid=(n // MIXPROJ_TILE,),
        in_specs=[tile, _resident(norm.shape), _resident(w.shape), _resident(lb.shape)],
        out_specs=[out_tile] * len(MIXPROJ_DTYPES),
        out_shape=[jax.ShapeDtypeStruct((n, PROJ_WIDTH), dt) for dt in MIXPROJ_DTYPES],
        compiler_params=_params("parallel"),
        name="mixproj",
    )(h, norm, w, lb)


def _rel_bucket(rel):
    nb = N_BUCKETS // 2
    max_exact = nb // 2
    side = jnp.where(rel > 0, nb, 0)
    n = jnp.abs(rel)
    nf = jnp.maximum(n, 1).astype(jnp.float32)
    large = max_exact + (jnp.log(nf / max_exact) / math.log(MAX_DISTANCE / max_exact)
                         * (nb - max_exact)).astype(jnp.int32)
    large = jnp.minimum(large, nb - 1)
    return side + jnp.where(n < max_exact, n, large)


def _bias_tables(rel_bias, seq):
    nb = N_BUCKETS // 2
    n_q = seq // Q_TILE
    assert 2 * n_q <= HEAD_DIM and KEY_CHUNK == Q_TILE and NEAR_REACH * KEY_CHUNK >= MAX_DISTANCE
    rb = rel_bias.astype(F32) * LOG2E
    width = (2 * NEAR_REACH + 3) * KEY_CHUNK
    own = (NEAR_REACH + 1) * KEY_CHUNK
    i = jnp.arange(Q_TILE, dtype=jnp.int32)[:, None]
    j = jnp.arange(width, dtype=jnp.int32)[None, :]
    bucket = _rel_bucket(j - own - i)
    band = jnp.zeros((ATTN_HEADS, Q_TILE, width), F32)
    for n in range(N_BUCKETS):
        band = jnp.where(bucket[None] == n, rb[n][:, None, None], band)
    before = rb[nb - 1][:, None, None]
    after = rb[2 * nb - 1][:, None, None]
    near = band - jnp.where(j[None] < own, before, after)
    key_pos = jnp.arange(seq, dtype=jnp.int32)[None, :, None]
    tile_start = (jnp.arange(n_q, dtype=jnp.int32) * Q_TILE)[None, None, :]
    far = jnp.where(key_pos < tile_start, before, after)
    hi = far.astype(BF16)
    lo = (far - hi.astype(F32)).astype(BF16)
    pad = jnp.zeros((ATTN_HEADS, seq, HEAD_DIM - 2 * n_q), BF16)
    return jnp.concatenate([hi, lo, pad], axis=-1), near


N_ATTN_UNITS = 2 * Q_TILES_PER_STEP


def _attn_body(lam_ref, q_ref, k_ref, v_ref, far_ref, near_ref, norm_ref, o_ref,
               kx_ref, vx_ref, qx_ref, m_ref, *unit_refs):
    step = pl.program_id(2)
    seq = k_ref.shape[1]
    n_q = seq // Q_TILE
    n_kc = seq // KEY_CHUNK
    s_refs = unit_refs[:N_ATTN_UNITS]
    p_refs = unit_refs[N_ATTN_UNITS:]

    @pl.when(step == 0)
    def _():
        kx_ref[:, :HEAD_DIM] = k_ref[0]
        kx_ref[:, HEAD_DIM:] = far_ref[0]
        vx_ref[:, :HEAD_DIM] = v_ref[0]
        vx_ref[:, HEAD_DIM:] = jnp.ones((seq, HEAD_DIM), BF16)

    lane = lax.broadcasted_iota(jnp.int32, (Q_TILE, HEAD_DIM), 1)
    zero = jnp.zeros((Q_TILE, HEAD_DIM), BF16)
    for t in range(Q_TILES_PER_STEP):
        q = q_ref[0, t * Q_TILE:(t + 1) * Q_TILE, :]
        tile = step * Q_TILES_PER_STEP + t
        tile_onehot = jnp.where((lane == tile) | (lane == n_q + tile), 1.0, 0.0).astype(BF16)
        qx_ref[2 * t, :, :HEAD_DIM] = jnp.where(lane < ATTN_HALF_DIM, q, zero)
        qx_ref[2 * t + 1, :, :HEAD_DIM] = jnp.where(lane >= ATTN_HALF_DIM, q, zero)
        qx_ref[2 * t, :, HEAD_DIM:] = tile_onehot
        qx_ref[2 * t + 1, :, HEAD_DIM:] = tile_onehot
    m_ref[...] = jnp.full(m_ref.shape, -jnp.inf, F32)

    chunks_per_iter = min(CHUNKS_PER_ITER, n_kc)
    assert n_kc % chunks_per_iter == 0

    def score_chunks(it, carry):
        for j in range(chunks_per_iter):
            kc = it * chunks_per_iter + j
            cols = pl.ds(pl.multiple_of(kc * KEY_CHUNK, KEY_CHUNK), KEY_CHUNK)
            for t in range(Q_TILES_PER_STEP):
                rel = jnp.clip(kc - (step * Q_TILES_PER_STEP + t), -NEAR_REACH - 1, NEAR_REACH + 1)
                near = near_ref[0, :, pl.ds(pl.multiple_of((rel + NEAR_REACH + 1) * KEY_CHUNK, KEY_CHUNK),
                                            KEY_CHUNK)]
                for u in (2 * t, 2 * t + 1):
                    s = lax.dot_general(qx_ref[u], kx_ref[cols, :], NT_DIMS,
                                        preferred_element_type=F32) + near
                    s_refs[u][:, cols] = s
                    m_ref[u] = jnp.maximum(m_ref[u], jnp.maximum(s[:, :LANES], s[:, LANES:]))
        return carry
    lax.fori_loop(0, n_kc // chunks_per_iter, score_chunks, 0)

    outs = []
    for u in range(N_ATTN_UNITS):
        m = jnp.max(m_ref[u], axis=-1, keepdims=True)
        for kc in range(n_kc):
            cols = slice(kc * KEY_CHUNK, (kc + 1) * KEY_CHUNK)
            p_refs[u][:, cols] = jnp.exp2((s_refs[u][:, cols] - m).astype(BF16))
        acc = _dot(p_refs[u][...], vx_ref[...])
        outs.append(acc[:, :HEAD_DIM] / acc[:, HEAD_DIM:])
    for t in range(Q_TILES_PER_STEP):
        o = outs[2 * t] - lam_ref[0] * outs[2 * t + 1]
        o_ref[0, t * Q_TILE:(t + 1) * Q_TILE, :] = (
            _rms(o, norm_ref[...]) * (1.0 - LAMBDA_INIT)).astype(o_ref.dtype)


def _attention(q, k, v, far, near, lam, head_norm):
    b, s, _ = q.shape
    rows = Q_TILES_PER_STEP * Q_TILE
    smem = pl.BlockSpec(memory_space=pltpu.SMEM)
    return pl.pallas_call(
        _attn_body,
        grid=(b, ATTN_HEADS, s // rows),
        in_specs=[smem,
                  pl.BlockSpec((1, rows, HEAD_DIM), lambda bi, h, i: (bi, i, h)),
                  pl.BlockSpec((1, s, HEAD_DIM), lambda bi, h, i: (bi, 0, h)),
                  pl.BlockSpec((1, s, HEAD_DIM), lambda bi, h, i: (bi, 0, h)),
                  pl.BlockSpec((1, s, HEAD_DIM), lambda bi, h, i: (h, 0, 0)),
                  pl.BlockSpec((1,) + near.shape[1:], lambda bi, h, i: (h, 0, 0)),
                  _resident(head_norm.shape)],
        out_specs=pl.BlockSpec((1, rows, HEAD_DIM), lambda bi, h, i: (bi, i, h)),
        out_shape=jax.ShapeDtypeStruct((b, s, ATTN_WIDTH), BF16),
        scratch_shapes=[pltpu.VMEM((s, 2 * HEAD_DIM), BF16), pltpu.VMEM((s, 2 * HEAD_DIM), BF16),
                        pltpu.VMEM((N_ATTN_UNITS, Q_TILE, 2 * HEAD_DIM), BF16),
                        pltpu.VMEM((N_ATTN_UNITS, Q_TILE, LANES), F32)]
                       + [pltpu.VMEM((Q_TILE, s), F32)] * N_ATTN_UNITS
                       + [pltpu.VMEM((Q_TILE, s), BF16)] * N_ATTN_UNITS,
        compiler_params=_params("parallel", "parallel", "arbitrary"),
        name="attn",
    )(lam, q, k, v, far, near, head_norm)


def _chunk_triangles():
    t = np.arange(RNN_GROUP)[:, None]
    s = np.arange(RNN_GROUP)[None, :]
    same = (t // CHUNK) == (s // CHUNK)
    return np.stack([same & (s <= t), same & (s >= t)]).astype(np.float32)


def _hgrn2_body(q_ref, v_ref, gate_ref, kf_ref, kb_ref, gf_ref, gb_ref, tri_ref, norm_ref, o_ref,
                acc_ref, qd_ref, qm_ref, km_ref, kd_ref, u_ref, dl_ref, sp_ref):
    seq = q_ref.shape[1]
    n_groups = seq // RNN_GROUP
    n_chunks = seq // CHUNK
    per_group = RNN_GROUP // CHUNK
    k_refs = (kf_ref, kb_ref)
    g_refs = (gf_ref, gb_ref)
    mid_row = (CHUNK // 2 - 1, CHUNK // 2)
    last_row = (CHUNK - 1, 0)

    def chunked(t):
        return t.reshape(per_group, CHUNK, HEAD_DIM)

    def flat(t):
        return t.reshape(RNN_GROUP, HEAD_DIM)

    def group_rows(gi):
        return pl.ds(pl.multiple_of(gi * RNN_GROUP, RNN_GROUP), RNN_GROUP)

    def decays(gi):
        rows = group_rows(gi)
        q = chunked(q_ref[0, rows, :].astype(F32))
        for d in range(2):
            log_f = g_refs[d][0, rows, :]
            hi = log_f.astype(BF16)
            lo = (log_f - hi.astype(F32)).astype(BF16)
            both = _dot(tri_ref[d], jnp.concatenate([hi, lo], axis=1))
            cum = chunked(both[:, :HEAD_DIM] + both[:, HEAD_DIM:])
            cum_mid = cum[:, mid_row[d]:mid_row[d] + 1, :]
            cum_last = cum[:, last_row[d]:last_row[d] + 1, :]
            k = chunked(k_refs[d][0, rows, :].astype(F32))
            q_mid = q * jnp.exp(cum - cum_mid)
            k_mid = k * jnp.exp(cum_mid - cum)
            lanes = slice(d * HEAD_DIM, (d + 1) * HEAD_DIM)
            qm_ref[d, rows, :] = flat(q_mid).astype(BF16)
            km_ref[d, rows, :] = flat(k_mid).astype(BF16)
            qd_ref[rows, lanes] = flat(q_mid * jnp.exp(cum_mid)).astype(BF16)
            kd_ref[rows, lanes] = flat(k_mid * jnp.exp(cum_last - cum_mid)).astype(BF16)
            decay = jnp.exp(cum_last)
            for j in range(per_group):
                dl_ref[d, gi * per_group + j] = decay[j]

    def products(gi):
        rows = group_rows(gi)
        v = v_ref[0, rows, :]
        a = None
        for d in range(2):
            a_d = lax.dot_general(qm_ref[d, rows, :], km_ref[d, rows, :], NT_DIMS,
                                  preferred_element_type=F32).astype(BF16)
            a_d = jnp.where(tri_ref[d] > 0, a_d, jnp.zeros_like(a_d))
            a = a_d if a is None else a + a_d
        acc_ref[rows, :] = _dot(a, v)
        k_dec = kd_ref[rows, :]
        for j in range(per_group):
            chunk = slice(j * CHUNK, (j + 1) * CHUNK)
            u = lax.dot_general(v[chunk], k_dec[chunk], TN_DIMS, preferred_element_type=F32)
            for d in range(2):
                u_ref[d, gi * per_group + j] = u[:, d * HEAD_DIM:(d + 1) * HEAD_DIM]

    decays(0)

    def chunk_local(gi, carry):
        products(gi - 1)
        decays(gi)
        return carry
    lax.fori_loop(1, n_groups, chunk_local, 0, unroll=3)
    products(n_groups - 1)

    def scan(i, states):
        new = []
        for d, state in enumerate(states):
            ci = i if d == 0 else n_chunks - 1 - i
            sp_ref[ci, :, d * HEAD_DIM:(d + 1) * HEAD_DIM] = state.astype(BF16)
            new.append(state * dl_ref[d, ci] + u_ref[d, ci])
        return tuple(new)
    zero = jnp.zeros((HEAD_DIM, HEAD_DIM), F32)
    lax.fori_loop(0, n_chunks, scan, (zero, zero))

    def outputs(gi, carry):
        rows = group_rows(gi)
        entering = [lax.dot_general(qd_ref[pl.ds(pl.multiple_of(gi * RNN_GROUP + j * CHUNK, CHUNK), CHUNK), :],
                                    sp_ref[gi * per_group + j], NT_DIMS,
                                    preferred_element_type=F32) for j in range(per_group)]
        total = acc_ref[rows, :] + jnp.concatenate(entering, axis=0)
        gate = gate_ref[0, rows, :].astype(F32)
        o_ref[0, rows, :] = (_rms(total, norm_ref[...]) * gate).astype(o_ref.dtype)
        return carry
    lax.fori_loop(0, n_groups, outputs, 0, unroll=4)


def _hgrn2(q, v, gate, k_fw, k_bw, g_fw, g_bw, head_norm):
    b, s, _ = q.shape
    n_chunks = s // CHUNK
    tri = jnp.asarray(_chunk_triangles(), BF16)
    seq_block = pl.BlockSpec((1, s, HEAD_DIM), lambda bi, h: (bi, 0, h))
    return pl.pallas_call(
        _hgrn2_body,
        grid=(b, RNN_HEADS),
        in_specs=[seq_block] * 7 + [_resident(tri.shape), _resident(head_norm.shape)],
        out_specs=seq_block,
        out_shape=jax.ShapeDtypeStruct((b, s, RNN_WIDTH), BF16),
        scratch_shapes=[pltpu.VMEM((s, HEAD_DIM), F32),
                        pltpu.VMEM((s, 2 * HEAD_DIM), BF16),
                        pltpu.VMEM((2, s, HEAD_DIM), BF16),
                        pltpu.VMEM((2, s, HEAD_DIM), BF16),
                        pltpu.VMEM((s, 2 * HEAD_DIM), BF16),
                        pltpu.VMEM((2, n_chunks, HEAD_DIM, HEAD_DIM), F32),
                        pltpu.VMEM((2, n_chunks, 1, HEAD_DIM), F32),
                        pltpu.VMEM((n_chunks, HEAD_DIM, 2 * HEAD_DIM), BF16)],
        compiler_params=_params("parallel", "parallel"),
        name="hgrn2",
    )(q, v, gate, k_fw, k_bw, g_fw, g_bw, tri, head_norm)


def kernel(x, rel_bias, lb_logits, ffn1_pre_norm, ffn1_w_in, ffn1_w_out, ffn1_post_norm, mix_pre_norm, w_mix_in, lambda_q1, lambda_k1, lambda_q2, lambda_k2, attn_head_norm, rnn_head_norm, w_mix_out, mix_post_norm, ffn2_pre_norm, ffn2_w_in, ffn2_w_out, ffn2_post_norm):
    b, s, d = x.shape
    layer = 0
    row = lambda t: t[layer].reshape(1, -1).astype(F32)

    h1 = _ffn(x.reshape(b * s, d), row(ffn1_pre_norm), ffn1_w_in[layer], ffn1_w_out[layer],
              row(ffn1_post_norm))

    lb = jnp.cumsum(jax.nn.softmax(lb_logits.astype(F32), axis=1), axis=1)[:, layer]
    q_a, k_a, v_a, q_r, i_r, g_r, k_fw, k_bw, g_fw, g_bw = _mixproj(
        h1, row(mix_pre_norm), w_mix_in[layer], lb.reshape(2, 1, RNN_WIDTH))
    seq = lambda t: t.reshape(b, s, t.shape[-1])

    lam = (jnp.exp(jnp.sum(lambda_q1[layer] * lambda_k1[layer]).astype(F32))
           - jnp.exp(jnp.sum(lambda_q2[layer] * lambda_k2[layer]).astype(F32))
           + LAMBDA_INIT).reshape(1)
    far, near = _bias_tables(rel_bias, s)
    o_a = _attention(seq(q_a), seq(k_a), seq(v_a), far, near, lam, row(attn_head_norm))

    o_r = _hgrn2(seq(q_r), seq(i_r), seq(g_r), seq(k_fw), seq(k_bw), seq(g_fw), seq(g_bw),
                 row(rnn_head_norm))

    y = _mix_ffn(h1, o_a.reshape(b * s, ATTN_WIDTH), o_r.reshape(b * s, RNN_WIDTH),
                 w_mix_out[layer], row(mix_post_norm), row(ffn2_pre_norm),
                 ffn2_w_in[layer], ffn2_w_out[layer], row(ffn2_post_norm))
    return y.reshape(b, s, d).astype(x.dtype)
```

```python
import functools
import math

import numpy as np
import jax
import jax.numpy as jnp
from jax import lax
from jax.experimental import pallas as pl
from jax.experimental.pallas import tpu as pltpu

F32 = jnp.float32
BF16 = jnp.bfloat16

D_MODEL = 1024
ATTN_WIDTH = 512
RNN_WIDTH = 512
ATTN_HEADS = 4
ATTN_HALF_DIM = 64
HEAD_DIM = 128
RNN_HEADS = 4
D_FF = 2816
N_BUCKETS = 32
MAX_DISTANCE = 128
CHUNK = 64
EPS = 1e-6
LAMBDA_INIT = 0.8 - 0.6 * math.exp(-0.3 * 0)

VMEM_LIMIT_BYTES = 56 * 1024 * 1024
LANES = 128

TOKEN_TILE = 512
MIXPROJ_TILE = 1024
FF_CHUNK = 256
Q_TILE = 256
Q_TILES_PER_STEP = 2
KEY_CHUNK = 256
NEAR_REACH = 1
LOG2E = math.log2(math.e)
RNN_GROUP = 256

NT_DIMS = (((1,), (1,)), ((), ()))
TN_DIMS = (((0,), (0,)), ((), ()))


def _rms(x, w):
    return x * lax.rsqrt(jnp.mean(x * x, axis=-1, keepdims=True) + EPS) * w


def _dot(a, b):
    return jnp.dot(a, b, preferred_element_type=F32)


def _silu(x):
    half = 0.5 * x
    return half + half * jnp.tanh(half)


def _resident(shape):
    return pl.BlockSpec(shape, lambda *_: (0,) * len(shape), pipeline_mode=pl.Buffered(1))


def _params(*semantics):
    return pltpu.CompilerParams(dimension_semantics=semantics, vmem_limit_bytes=VMEM_LIMIT_BYTES)


def _swiglu_half_step(h, pre_ref, w_in_ref, w_out_ref, post_ref, acc_ref):
    u = _rms(h, pre_ref[...]).astype(BF16)
    for c in range(D_FF // FF_CHUNK):
        lo = c * FF_CHUNK
        gate = _dot(u, w_in_ref[:, lo:lo + FF_CHUNK].astype(BF16))
        up = _dot(u, w_in_ref[:, D_FF + lo:D_FF + lo + FF_CHUNK].astype(BF16))
        act = (_silu(gate) * up).astype(BF16)
        part = _dot(act, w_out_ref[lo:lo + FF_CHUNK, :].astype(BF16))
        if c == 0:
            acc_ref[...] = part
        else:
            acc_ref[...] += part
    return h + 0.5 * _rms(acc_ref[...], post_ref[...])


def _ffn_body(x_ref, pre_ref, w_in_ref, w_out_ref, post_ref, o_ref, acc_ref):
    o_ref[...] = _swiglu_half_step(x_ref[...], pre_ref, w_in_ref, w_out_ref, post_ref, acc_ref)


def _ffn(x, pre, w_in, w_out, post):
    n, d = x.shape
    tile = pl.BlockSpec((TOKEN_TILE, d), lambda i: (i, 0))
    return pl.pallas_call(
        _ffn_body,
        grid=(n // TOKEN_TILE,),
        in_specs=[tile, _resident(pre.shape), _resident(w_in.shape), _resident(w_out.shape),
                  _resident(post.shape)],
        out_specs=tile,
        out_shape=jax.ShapeDtypeStruct((n, d), F32),
        scratch_shapes=[pltpu.VMEM((TOKEN_TILE, d), F32)],
        compiler_params=_params("parallel"),
        name="ffn",
    )(x, pre, w_in, w_out, post)


def _mix_ffn_body(h_ref, oa_ref, or_ref, w_mix_ref, mix_post_ref, pre_ref, w_in_ref, w_out_ref,
                  post_ref, o_ref, acc_ref):
    mixed = (_dot(oa_ref[...], w_mix_ref[:ATTN_WIDTH, :].astype(BF16))
             + _dot(or_ref[...], w_mix_ref[ATTN_WIDTH:, :].astype(BF16)))
    h = h_ref[...] + _rms(mixed, mix_post_ref[...])
    o_ref[...] = _swiglu_half_step(h, pre_ref, w_in_ref, w_out_ref, post_ref, acc_ref)


def _mix_ffn(h, o_a, o_r, w_mix, mix_post, pre, w_in, w_out, post):
    n, d = h.shape
    tile = pl.BlockSpec((TOKEN_TILE, d), lambda i: (i, 0))
    half = pl.BlockSpec((TOKEN_TILE, ATTN_WIDTH), lambda i: (i, 0))
    return pl.pallas_call(
        _mix_ffn_body,
        grid=(n // TOKEN_TILE,),
        in_specs=[tile, half, half, _resident(w_mix.shape), _resident(mix_post.shape),
                  _resident(pre.shape), _resident(w_in.shape), _resident(w_out.shape),
                  _resident(post.shape)],
        out_specs=tile,
        out_shape=jax.ShapeDtypeStruct((n, d), F32),
        scratch_shapes=[pltpu.VMEM((TOKEN_TILE, d), F32)],
        compiler_params=_params("parallel"),
        name="mix_ffn",
    )(h, o_a, o_r, w_mix, mix_post, pre, w_in, w_out, post)


PROJ_PART = 256
PROJ_WIDTH = 512
MIXPROJ_DTYPES = [BF16] * 8 + [F32] * 2


def _mixproj_body(h_ref, norm_ref, w_ref, lb_ref, qa_ref, ka_ref, va_ref, qr_ref, ir_ref, gr_ref,
                  kf_ref, kb_ref, gf_ref, gb_ref):
    u = _rms(h_ref[...], norm_ref[...]).astype(BF16)

    def silu_store(o_ref, cols, x):
        o_ref[:, cols] = _silu(x).astype(BF16)

    def gate_store(direction, k_ref, g_ref, cols, z):
        lb = lb_ref[direction][:, cols]
        c1 = 0.5 * (1.0 - lb)
        ct = c1 * jnp.tanh(0.5 * z)
        k_ref[:, cols] = (c1 - ct).astype(BF16)
        g_ref[:, cols] = jnp.log((lb + c1) + ct)

    def cast_store(o_ref, cols, x):
        o_ref[:, cols] = x.astype(BF16)

    def scaled_store(o_ref, cols, x):
        o_ref[:, cols] = (x * (ATTN_HALF_DIM ** -0.5 * LOG2E)).astype(BF16)

    plan = [(5, functools.partial(gate_store, 0, kf_ref, gf_ref)),
            (6, functools.partial(gate_store, 1, kb_ref, gb_ref)),
            (3, functools.partial(silu_store, qr_ref)),
            (7, functools.partial(silu_store, gr_ref)),
            (0, functools.partial(scaled_store, qa_ref)),
            (1, functools.partial(cast_store, ka_ref)),
            (2, functools.partial(cast_store, va_ref)),
            (4, functools.partial(cast_store, ir_ref))]
    for j, store in plan:
        for part in range(PROJ_WIDTH // PROJ_PART):
            cols = slice(part * PROJ_PART, (part + 1) * PROJ_PART)
            lo = j * PROJ_WIDTH + part * PROJ_PART
            store(cols, _dot(u, w_ref[:, lo:lo + PROJ_PART].astype(BF16)))


def _mixproj(h, norm, w, lb):
    n, d = h.shape
    tile = pl.BlockSpec((MIXPROJ_TILE, d), lambda i: (i, 0))
    out_tile = pl.BlockSpec((MIXPROJ_TILE, PROJ_WIDTH), lambda i: (i, 0))
    return pl.pallas_call(
        _mixproj_body,
        grid=(n // MIXPROJ_TILE,),
        in_specs=[tile, _resident(norm.shape), _resident(w.shape), _resident(lb.shape)],
        out_specs=[out_tile] * len(MIXPROJ_DTYPES),
        out_shape=[jax.ShapeDtypeStruct((n, PROJ_WIDTH), dt) for dt in MIXPROJ_DTYPES],
        compiler_params=_params("parallel"),
        name="mixproj",
    )(h, norm, w, lb)


def _rel_bucket(rel):
    nb = N_BUCKETS // 2
    max_exact = nb // 2
    side = jnp.where(rel > 0, nb, 0)
    n = jnp.abs(rel)
    nf = jnp.maximum(n, 1).astype(jnp.float32)
    large = max_exact + (jnp.log(nf / max_exact) / math.log(MAX_DISTANCE / max_exact)
                         * (nb - max_exact)).astype(jnp.int32)
    large = jnp.minimum(large, nb - 1)
    return side + jnp.where(n < max_exact, n, large)


def _bias_tables(rel_bias, seq):
    nb = N_BUCKETS // 2
    n_q = seq // Q_TILE
    assert 2 * n_q <= HEAD_DIM and KEY_CHUNK == Q_TILE and NEAR_REACH * KEY_CHUNK >= MAX_DISTANCE
    rb = rel_bias.astype(F32) * LOG2E
    width = (2 * NEAR_REACH + 3) * KEY_CHUNK
    own = (NEAR_REACH + 1) * KEY_CHUNK
    i = jnp.arange(Q_TILE, dtype=jnp.int32)[:, None]
    j = jnp.arange(width, dtype=jnp.int32)[None, :]
    bucket = _rel_bucket(j - own - i)
    band = jnp.zeros((ATTN_HEADS, Q_TILE, width), F32)
    for n in range(N_BUCKETS):
        band = jnp.where(bucket[None] == n, rb[n][:, None, None], band)
    before = rb[nb - 1][:, None, None]
    after = rb[2 * nb - 1][:, None, None]
    near = band - jnp.where(j[None] < own, before, after)
    key_pos = jnp.arange(seq, dtype=jnp.int32)[None, :, None]
    tile_start = (jnp.arange(n_q, dtype=jnp.int32) * Q_TILE)[None, None, :]
    far = jnp.where(key_pos < tile_start, before, after)
    hi = far.astype(BF16)
    lo = (far - hi.astype(F32)).astype(BF16)
    pad = jnp.zeros((ATTN_HEADS, seq, HEAD_DIM - 2 * n_q), BF16)
    return jnp.concatenate([hi, lo, pad], axis=-1), near


N_ATTN_UNITS = 2 * Q_TILES_PER_STEP


def _attn_body(lam_ref, q_ref, k_ref, v_ref, far_ref, near_ref, norm_ref, o_ref,
               kx_ref, vx_ref, qx_ref, m_ref, *unit_refs):
    step = pl.program_id(2)
    seq = k_ref.shape[1]
    n_q = seq // Q_TILE
    n_kc = seq // KEY_CHUNK
    s_refs = unit_refs[:N_ATTN_UNITS]
    p_refs = unit_refs[N_ATTN_UNITS:]

    @pl.when(step == 0)
    def _():
        kx_ref[:, :HEAD_DIM] = k_ref[0]
        kx_ref[:, HEAD_DIM:] = far_ref[0]
        vx_ref[:, :HEAD_DIM] = v_ref[0]
        vx_ref[:, HEAD_DIM:] = jnp.ones((seq, HEAD_DIM), BF16)

    lane = lax.broadcasted_iota(jnp.int32, (Q_TILE, HEAD_DIM), 1)
    zero = jnp.zeros((Q_TILE, HEAD_DIM), BF16)
    for t in range(Q_TILES_PER_STEP):
        q = q_ref[0, t * Q_TILE:(t + 1) * Q_TILE, :]
        tile = step * Q_TILES_PER_STEP + t
        tile_onehot = jnp.where((lane == tile) | (lane == n_q + tile), 1.0, 0.0).astype(BF16)
        qx_ref[2 * t, :, :HEAD_DIM] = jnp.where(lane < ATTN_HALF_DIM, q, zero)
        qx_ref[2 * t + 1, :, :HEAD_DIM] = jnp.where(lane >= ATTN_HALF_DIM, q, zero)
        qx_ref[2 * t, :, HEAD_DIM:] = tile_onehot
        qx_ref[2 * t + 1, :, HEAD_DIM:] = tile_onehot
    m_ref[...] = jnp.full(m_ref.shape, -jnp.inf, F32)

    def scores(t):
        for kc in range(n_kc):
            cols = slice(kc * KEY_CHUNK, (kc + 1) * KEY_CHUNK)
            rel = jnp.clip(kc - (step * Q_TILES_PER_STEP + t), -NEAR_REACH - 1, NEAR_REACH + 1)
            near = near_ref[0, :, pl.ds(pl.multiple_of((rel + NEAR_REACH + 1) * KEY_CHUNK, KEY_CHUNK),
                                        KEY_CHUNK)]
            for u in (2 * t, 2 * t + 1):
                s = lax.dot_general(qx_ref[u], kx_ref[cols, :], NT_DIMS,
                                    preferred_element_type=F32) + near
                s_refs[u][:, cols] = s
                m_ref[u] = jnp.maximum(m_ref[u], jnp.maximum(s[:, :LANES], s[:, LANES:]))

    def weighted(t):
        outs = []
        for u in (2 * t, 2 * t + 1):
            m = jnp.max(m_ref[u], axis=-1, keepdims=True)
            for kc in range(n_kc):
                cols = slice(kc * KEY_CHUNK, (kc + 1) * KEY_CHUNK)
                p_refs[u][:, cols] = jnp.exp2((s_refs[u][:, cols] - m).astype(BF16))
            acc = _dot(p_refs[u][...], vx_ref[...])
            outs.append(acc[:, :HEAD_DIM] / acc[:, HEAD_DIM:])
        o = outs[0] - lam_ref[0] * outs[1]
        o_ref[0, t * Q_TILE:(t + 1) * Q_TILE, :] = (
            _rms(o, norm_ref[...]) * (1.0 - LAMBDA_INIT)).astype(o_ref.dtype)

    scores(0)
    for t in range(1, Q_TILES_PER_STEP + 1):
        @pl.when(step >= 0)
        def _(t=t):
            if t < Q_TILES_PER_STEP:
                scores(t)
            weighted(t - 1)


def _attention(q, k, v, far, near, lam, head_norm):
    b, s, _ = q.shape
    rows = Q_TILES_PER_STEP * Q_TILE
    smem = pl.BlockSpec(memory_space=pltpu.SMEM)
    return pl.pallas_call(
        _attn_body,
        grid=(b, ATTN_HEADS, s // rows),
        in_specs=[smem,
                  pl.BlockSpec((1, rows, HEAD_DIM), lambda bi, h, i: (bi, i, h)),
                  pl.BlockSpec((1, s, HEAD_DIM), lambda bi, h, i: (bi, 0, h)),
                  pl.BlockSpec((1, s, HEAD_DIM), lambda bi, h, i: (bi, 0, h)),
                  pl.BlockSpec((1, s, HEAD_DIM), lambda bi, h, i: (h, 0, 0)),
                  pl.BlockSpec((1,) + near.shape[1:], lambda bi, h, i: (h, 0, 0)),
                  _resident(head_norm.shape)],
        out_specs=pl.BlockSpec((1, rows, HEAD_DIM), lambda bi, h, i: (bi, i, h)),
        out_shape=jax.ShapeDtypeStruct((b, s, ATTN_WIDTH), BF16),
        scratch_shapes=[pltpu.VMEM((s, 2 * HEAD_DIM), BF16), pltpu.VMEM((s, 2 * HEAD_DIM), BF16),
                        pltpu.VMEM((N_ATTN_UNITS, Q_TILE, 2 * HEAD_DIM), BF16),
                        pltpu.VMEM((N_ATTN_UNITS, Q_TILE, LANES), F32)]
                       + [pltpu.VMEM((Q_TILE, s), F32)] * N_ATTN_UNITS
                       + [pltpu.VMEM((Q_TILE, s), BF16)] * N_ATTN_UNITS,
        compiler_params=_params("parallel", "parallel", "arbitrary"),
        name="attn",
    )(lam, q, k, v, far, near, head_norm)


def _chunk_triangles():
    t = np.arange(RNN_GROUP)[:, None]
    s = np.arange(RNN_GROUP)[None, :]
    same = (t // CHUNK) == (s // CHUNK)
    return np.stack([same & (s <= t), same & (s >= t)]).astype(np.float32)


def _hgrn2_body(q_ref, v_ref, gate_ref, kf_ref, kb_ref, gf_ref, gb_ref, tri_ref, norm_ref, o_ref,
                acc_ref, qd_ref, qm_ref, km_ref, kd_ref, u_ref, dl_ref, sp_ref):
    seq = q_ref.shape[1]
    n_groups = seq // RNN_GROUP
    n_chunks = seq // CHUNK
    per_group = RNN_GROUP // CHUNK
    k_refs = (kf_ref, kb_ref)
    g_refs = (gf_ref, gb_ref)
    mid_row = (CHUNK // 2 - 1, CHUNK // 2)
    last_row = (CHUNK - 1, 0)

    def chunked(t):
        return t.reshape(per_group, CHUNK, HEAD_DIM)

    def flat(t):
        return t.reshape(RNN_GROUP, HEAD_DIM)

    def group_rows(gi):
        return pl.ds(pl.multiple_of(gi * RNN_GROUP, RNN_GROUP), RNN_GROUP)

    def decays(gi):
        rows = group_rows(gi)
        q = chunked(q_ref[0, rows, :].astype(F32))
        for d in range(2):
            log_f = g_refs[d][0, rows, :]
            hi = log_f.astype(BF16)
            lo = (log_f - hi.astype(F32)).astype(BF16)
            both = _dot(tri_ref[d], jnp.concatenate([hi, lo], axis=1))
            cum = chunked(both[:, :HEAD_DIM] + both[:, HEAD_DIM:])
            cum_mid = cum[:, mid_row[d]:mid_row[d] + 1, :]
            cum_last = cum[:, last_row[d]:last_row[d] + 1, :]
            k = chunked(k_refs[d][0, rows, :].astype(F32))
            q_mid = q * jnp.exp(cum - cum_mid)
            k_mid = k * jnp.exp(cum_mid - cum)
            lanes = slice(d * HEAD_DIM, (d + 1) * HEAD_DIM)
            qm_ref[d, rows, :] = flat(q_mid).astype(BF16)
            km_ref[d, rows, :] = flat(k_mid).astype(BF16)
            qd_ref[rows, lanes] = flat(q_mid * jnp.exp(cum_mid)).astype(BF16)
            kd_ref[rows, lanes] = flat(k_mid * jnp.exp(cum_last - cum_mid)).astype(BF16)
            decay = jnp.exp(cum_last)
            for j in range(per_group):
                dl_ref[d, gi * per_group + j] = decay[j]

    def products(gi):
        rows = group_rows(gi)
        v = v_ref[0, rows, :]
        a = None
        for d in range(2):
            a_d = lax.dot_general(qm_ref[d, rows, :], km_ref[d, rows, :], NT_DIMS,
                                  preferred_element_type=F32).astype(BF16)
            a_d = jnp.where(tri_ref[d] > 0, a_d, jnp.zeros_like(a_d))
            a = a_d if a is None else a + a_d
        acc_ref[rows, :] = _dot(a, v)
        k_dec = kd_ref[rows, :]
        for j in range(per_group):
            chunk = slice(j * CHUNK, (j + 1) * CHUNK)
            u = lax.dot_general(v[chunk], k_dec[chunk], TN_DIMS, preferred_element_type=F32)
            for d in range(2):
                u_ref[d, gi * per_group + j] = u[:, d * HEAD_DIM:(d + 1) * HEAD_DIM]

    decays(0)

    def chunk_local(gi, carry):
        products(gi - 1)
        decays(gi)
        return carry
    lax.fori_loop(1, n_groups, chunk_local, 0, unroll=3)
    products(n_groups - 1)

    def scan(i, states):
        new = []
        for d, state in enumerate(states):
            ci = i if d == 0 else n_chunks - 1 - i
            sp_ref[ci, :, d * HEAD_DIM:(d + 1) * HEAD_DIM] = state.astype(BF16)
            new.append(state * dl_ref[d, ci] + u_ref[d, ci])
        return tuple(new)
    zero = jnp.zeros((HEAD_DIM, HEAD_DIM), F32)
    lax.fori_loop(0, n_chunks, scan, (zero, zero))

    def outputs(gi, carry):
        rows = group_rows(gi)
        entering = [lax.dot_general(qd_ref[pl.ds(pl.multiple_of(gi * RNN_GROUP + j * CHUNK, CHUNK), CHUNK), :],
                                    sp_ref[gi * per_group + j], NT_DIMS,
                                    preferred_element_type=F32) for j in range(per_group)]
        total = acc_ref[rows, :] + jnp.concatenate(entering, axis=0)
        gate = gate_ref[0, rows, :].astype(F32)
        o_ref[0, rows, :] = (_rms(total, norm_ref[...]) * gate).astype(o_ref.dtype)
        return carry
    lax.fori_loop(0, n_groups, outputs, 0, unroll=4)


def _hgrn2(q, v, gate, k_fw, k_bw, g_fw, g_bw, head_norm):
    b, s, _ = q.shape
    n_chunks = s // CHUNK
    tri = jnp.asarray(_chunk_triangles(), BF16)
    seq_block = pl.BlockSpec((1, s, HEAD_DIM), lambda bi, h: (bi, 0, h))
    return pl.pallas_call(
        _hgrn2_body,
        grid=(b, RNN_HEADS),
        in_specs=[seq_block] * 7 + [_resident(tri.shape), _resident(head_norm.shape)],
        out_specs=seq_block,
        out_shape=jax.ShapeDtypeStruct((b, s, RNN_WIDTH), BF16),
        scratch_shapes=[pltpu.VMEM((s, HEAD_DIM), F32),
                        pltpu.VMEM((s, 2 * HEAD_DIM), BF16),
                        pltpu.VMEM((2, s, HEAD_DIM), BF16),
                        pltpu.VMEM((2, s, HEAD_DIM), BF16),
                        pltpu.VMEM((s, 2 * HEAD_DIM), BF16),
                        pltpu.VMEM((2, n_chunks, HEAD_DIM, HEAD_DIM), F32),
                        pltpu.VMEM((2, n_chunks, 1, HEAD_DIM), F32),
                        pltpu.VMEM((n_chunks, HEAD_DIM, 2 * HEAD_DIM), BF16)],
        compiler_params=_params("parallel", "parallel"),
        name="hgrn2",
    )(q, v, gate, k_fw, k_bw, g_fw, g_bw, tri, head_norm)


def kernel(x, rel_bias, lb_logits, ffn1_pre_norm, ffn1_w_in, ffn1_w_out, ffn1_post_norm, mix_pre_norm, w_mix_in, lambda_q1, lambda_k1, lambda_q2, lambda_k2, attn_head_norm, rnn_head_norm, w_mix_out, mix_post_norm, ffn2_pre_norm, ffn2_w_in, ffn2_w_out, ffn2_post_norm):
    b, s, d = x.shape
    layer = 0
    row = lambda t: t[layer].reshape(1, -1).astype(F32)

    h1 = _ffn(x.reshape(b * s, d), row(ffn1_pre_norm), ffn1_w_in[layer], ffn1_w_out[layer],
              row(ffn1_post_norm))

    lb = jnp.cumsum(jax.nn.softmax(lb_logits.astype(F32), axis=1), axis=1)[:, layer]
    q_a, k_a, v_a, q_r, i_r, g_r, k_fw, k_bw, g_fw, g_bw = _mixproj(
        h1, row(mix_pre_norm), w_mix_in[layer], lb.reshape(2, 1, RNN_WIDTH))
    seq = lambda t: t.reshape(b, s, t.shape[-1])

    lam = (jnp.exp(jnp.sum(lambda_q1[layer] * lambda_k1[layer]).astype(F32))
           - jnp.exp(jnp.sum(lambda_q2[layer] * lambda_k2[layer]).astype(F32))
           + LAMBDA_INIT).reshape(1)
    far, near = _bias_tables(rel_bias, s)
    o_a = _attention(seq(q_a), seq(k_a), seq(v_a), far, near, lam, row(attn_head_norm))

    o_r = _hgrn2(seq(q_r), seq(i_r), seq(g_r), seq(k_fw), seq(k_bw), seq(g_fw), seq(g_bw),
                 row(rnn_head_norm))

    y = _mix_ffn(h1, o_a.reshape(b * s, ATTN_WIDTH), o_r.reshape(b * s, RNN_WIDTH),
                 w_mix_out[layer], row(mix_post_norm), row(ffn2_pre_norm),
                 ffn2_w_in[layer], ffn2_w_out[layer], row(ffn2_post_norm))
    return y.reshape(b, s, d).astype(x.dtype)
```

```python
import functools
import math

import numpy as np
import jax
import jax.numpy as jnp
from jax import lax
from jax.experimental import pallas as pl
from jax.experimental.pallas import tpu as pltpu

F32 = jnp.float32
BF16 = jnp.bfloat16

D_MODEL = 1024
ATTN_WIDTH = 512
RNN_WIDTH = 512
ATTN_HEADS = 4
ATTN_HALF_DIM = 64
HEAD_DIM = 128
RNN_HEADS = 4
D_FF = 2816
N_BUCKETS = 32
MAX_DISTANCE = 128
CHUNK = 64
EPS = 1e-6
LAMBDA_INIT = 0.8 - 0.6 * math.exp(-0.3 * 0)

VMEM_LIMIT_BYTES = 56 * 1024 * 1024
LANES = 128

TOKEN_TILE = 512
MIXPROJ_TILE = 1024
FF_CHUNK = 256
Q_TILE = 256
Q_TILES_PER_STEP = 8
KEY_CHUNK = 256
NEAR_REACH = 1
LOG2E = math.log2(math.e)
RNN_GROUP = 256

NT_DIMS = (((1,), (1,)), ((), ()))
TN_DIMS = (((0,), (0,)), ((), ()))


def _rms(x, w):
    return x * lax.rsqrt(jnp.mean(x * x, axis=-1, keepdims=True) + EPS) * w


def _dot(a, b):
    return jnp.dot(a, b, preferred_element_type=F32)


def _silu(x):
    half = 0.5 * x
    return half + half * jnp.tanh(half)


def _resident(shape):
    return pl.BlockSpec(shape, lambda *_: (0,) * len(shape), pipeline_mode=pl.Buffered(1))


def _params(*semantics):
    return pltpu.CompilerParams(dimension_semantics=semantics, vmem_limit_bytes=VMEM_LIMIT_BYTES)


def _swiglu_half_step(h, pre_ref, w_in_ref, w_out_ref, post_ref, acc_ref):
    u = _rms(h, pre_ref[...]).astype(BF16)
    for c in range(D_FF // FF_CHUNK):
        lo = c * FF_CHUNK
        gate = _dot(u, w_in_ref[:, lo:lo + FF_CHUNK].astype(BF16))
        up = _dot(u, w_in_ref[:, D_FF + lo:D_FF + lo + FF_CHUNK].astype(BF16))
        act = (_silu(gate) * up).astype(BF16)
        part = _dot(act, w_out_ref[lo:lo + FF_CHUNK, :].astype(BF16))
        if c == 0:
            acc_ref[...] = part
        else:
            acc_ref[...] += part
    return h + 0.5 * _rms(acc_ref[...], post_ref[...])


def _ffn_body(x_ref, pre_ref, w_in_ref, w_out_ref, post_ref, o_ref, acc_ref):
    o_ref[...] = _swiglu_half_step(x_ref[...], pre_ref, w_in_ref, w_out_ref, post_ref, acc_ref)


def _ffn(x, pre, w_in, w_out, post):
    n, d = x.shape
    tile = pl.BlockSpec((TOKEN_TILE, d), lambda i: (i, 0))
    return pl.pallas_call(
        _ffn_body,
        grid=(n // TOKEN_TILE,),
        in_specs=[tile, _resident(pre.shape), _resident(w_in.shape), _resident(w_out.shape),
                  _resident(post.shape)],
        out_specs=tile,
        out_shape=jax.ShapeDtypeStruct((n, d), F32),
        scratch_shapes=[pltpu.VMEM((TOKEN_TILE, d), F32)],
        compiler_params=_params("parallel"),
        name="ffn",
    )(x, pre, w_in, w_out, post)


def _mix_ffn_body(h_ref, oa_ref, or_ref, w_mix_ref, mix_post_ref, pre_ref, w_in_ref, w_out_ref,
                  post_ref, o_ref, acc_ref):
    mixed = (_dot(oa_ref[...], w_mix_ref[:ATTN_WIDTH, :].astype(BF16))
             + _dot(or_ref[...], w_mix_ref[ATTN_WIDTH:, :].astype(BF16)))
    h = h_ref[...] + _rms(mixed, mix_post_ref[...])
    o_ref[...] = _swiglu_half_step(h, pre_ref, w_in_ref, w_out_ref, post_ref, acc_ref)


def _mix_ffn(h, o_a, o_r, w_mix, mix_post, pre, w_in, w_out, post):
    n, d = h.shape
    tile = pl.BlockSpec((TOKEN_TILE, d), lambda i: (i, 0))
    half = pl.BlockSpec((TOKEN_TILE, ATTN_WIDTH), lambda i: (i, 0))
    return pl.pallas_call(
        _mix_ffn_body,
        grid=(n // TOKEN_TILE,),
        in_specs=[tile, half, half, _resident(w_mix.shape), _resident(mix_post.shape),
                  _resident(pre.shape), _resident(w_in.shape), _resident(w_out.shape),
                  _resident(post.shape)],
        out_specs=tile,
        out_shape=jax.ShapeDtypeStruct((n, d), F32),
        scratch_shapes=[pltpu.VMEM((TOKEN_TILE, d), F32)],
        compiler_params=_params("parallel"),
        name="mix_ffn",
    )(h, o_a, o_r, w_mix, mix_post, pre, w_in, w_out, post)


PROJ_PART = 256
PROJ_WIDTH = 512
MIXPROJ_DTYPES = [BF16] * 8 + [F32] * 2


def _mixproj_body(h_ref, norm_ref, w_ref, lb_ref, qa_ref, ka_ref, va_ref, qr_ref, ir_ref, gr_ref,
                  kf_ref, kb_ref, gf_ref, gb_ref):
    u = _rms(h_ref[...], norm_ref[...]).astype(BF16)

    def silu_store(o_ref, cols, x):
        o_ref[:, cols] = _silu(x).astype(BF16)

    def gate_store(direction, k_ref, g_ref, cols, z):
        lb = lb_ref[direction][:, cols]
        c1 = 0.5 * (1.0 - lb)
        ct = c1 * jnp.tanh(0.5 * z)
        k_ref[:, cols] = (c1 - ct).astype(BF16)
        g_ref[:, cols] = jnp.log((lb + c1) + ct)

    def cast_store(o_ref, cols, x):
        o_ref[:, cols] = x.astype(BF16)

    def scaled_store(o_ref, cols, x):
        o_ref[:, cols] = (x * (ATTN_HALF_DIM ** -0.5 * LOG2E)).astype(BF16)

    plan = [(5, functools.partial(gate_store, 0, kf_ref, gf_ref)),
            (6, functools.partial(gate_store, 1, kb_ref, gb_ref)),
            (3, functools.partial(silu_store, qr_ref)),
            (7, functools.partial(silu_store, gr_ref)),
            (0, functools.partial(scaled_store, qa_ref)),
            (1, functools.partial(cast_store, ka_ref)),
            (2, functools.partial(cast_store, va_ref)),
            (4, functools.partial(cast_store, ir_ref))]
    for j, store in plan:
        for part in range(PROJ_WIDTH // PROJ_PART):
            cols = slice(part * PROJ_PART, (part + 1) * PROJ_PART)
            lo = j * PROJ_WIDTH + part * PROJ_PART
            store(cols, _dot(u, w_ref[:, lo:lo + PROJ_PART].astype(BF16)))


def _mixproj(h, norm, w, lb):
    n, d = h.shape
    tile = pl.BlockSpec((MIXPROJ_TILE, d), lambda i: (i, 0))
    out_tile = pl.BlockSpec((MIXPROJ_TILE, PROJ_WIDTH), lambda i: (i, 0))
    return pl.pallas_call(
        _mixproj_body,
        grid=(n // MIXPROJ_TILE,),
        in_specs=[tile, _resident(norm.shape), _resident(w.shape), _resident(lb.shape)],
        out_specs=[out_tile] * len(MIXPROJ_DTYPES),
        out_shape=[jax.ShapeDtypeStruct((n, PROJ_WIDTH), dt) for dt in MIXPROJ_DTYPES],
        compiler_params=_params("parallel"),
        name="mixproj",
    )(h, norm, w, lb)


def _rel_bucket(rel):
    nb = N_BUCKETS // 2
    max_exact = nb // 2
    side = jnp.where(rel > 0, nb, 0)
    n = jnp.abs(rel)
    nf = jnp.maximum(n, 1).astype(jnp.float32)
    large = max_exact + (jnp.log(nf / max_exact) / math.log(MAX_DISTANCE / max_exact)
                         * (nb - max_exact)).astype(jnp.int32)
    large = jnp.minimum(large, nb - 1)
    return side + jnp.where(n < max_exact, n, large)


def _bias_tables(rel_bias, seq):
    nb = N_BUCKETS // 2
    n_q = seq // Q_TILE
    assert 2 * n_q <= HEAD_DIM and KEY_CHUNK == Q_TILE and NEAR_REACH * KEY_CHUNK >= MAX_DISTANCE
    rb = rel_bias.astype(F32) * LOG2E
    width = (2 * NEAR_REACH + 3) * KEY_CHUNK
    own = (NEAR_REACH + 1) * KEY_CHUNK
    i = jnp.arange(Q_TILE, dtype=jnp.int32)[:, None]
    j = jnp.arange(width, dtype=jnp.int32)[None, :]
    bucket = _rel_bucket(j - own - i)
    band = jnp.zeros((ATTN_HEADS, Q_TILE, width), F32)
    for n in range(N_BUCKETS):
        band = jnp.where(bucket[None] == n, rb[n][:, None, None], band)
    before = rb[nb - 1][:, None, None]
    after = rb[2 * nb - 1][:, None, None]
    near = band - jnp.where(j[None] < own, before, after)
    key_pos = jnp.arange(seq, dtype=jnp.int32)[None, :, None]
    tile_start = (jnp.arange(n_q, dtype=jnp.int32) * Q_TILE)[None, None, :]
    far = jnp.where(key_pos < tile_start, before, after)
    hi = far.astype(BF16)
    lo = (far - hi.astype(F32)).astype(BF16)
    pad = jnp.zeros((ATTN_HEADS, seq, HEAD_DIM - 2 * n_q), BF16)
    return jnp.concatenate([hi, lo, pad], axis=-1), near


MAPS = 2
N_ATTN_UNITS = MAPS * Q_TILES_PER_STEP


def _attn_body(lam_ref, q_ref, k_ref, v_ref, far_ref, near_ref, norm_ref, o_ref,
               kx_ref, vx_ref, qx_ref, m_ref, *unit_refs):
    step = pl.program_id(2)
    seq = k_ref.shape[1]
    n_q = seq // Q_TILE
    n_kc = seq // KEY_CHUNK
    s_refs = unit_refs[:2 * MAPS]
    p_refs = unit_refs[2 * MAPS:]

    @pl.when(step == 0)
    def _():
        kx_ref[:, :HEAD_DIM] = k_ref[0]
        kx_ref[:, HEAD_DIM:] = far_ref[0]
        vx_ref[:, :HEAD_DIM] = v_ref[0]
        vx_ref[:, HEAD_DIM:] = jnp.ones((seq, HEAD_DIM), BF16)

    lane = lax.broadcasted_iota(jnp.int32, (Q_TILE, HEAD_DIM), 1)
    zero = jnp.zeros((Q_TILE, HEAD_DIM), BF16)
    for t in range(Q_TILES_PER_STEP):
        q = q_ref[0, t * Q_TILE:(t + 1) * Q_TILE, :]
        tile = step * Q_TILES_PER_STEP + t
        tile_onehot = jnp.where((lane == tile) | (lane == n_q + tile), 1.0, 0.0).astype(BF16)
        qx_ref[2 * t, :, :HEAD_DIM] = jnp.where(lane < ATTN_HALF_DIM, q, zero)
        qx_ref[2 * t + 1, :, :HEAD_DIM] = jnp.where(lane >= ATTN_HALF_DIM, q, zero)
        qx_ref[2 * t, :, HEAD_DIM:] = tile_onehot
        qx_ref[2 * t + 1, :, HEAD_DIM:] = tile_onehot
    m_ref[...] = jnp.full(m_ref.shape, -jnp.inf, F32)

    def scores(t):
        for kc in range(n_kc):
            cols = slice(kc * KEY_CHUNK, (kc + 1) * KEY_CHUNK)
            rel = jnp.clip(kc - (step * Q_TILES_PER_STEP + t), -NEAR_REACH - 1, NEAR_REACH + 1)
            near = near_ref[0, :, pl.ds(pl.multiple_of((rel + NEAR_REACH + 1) * KEY_CHUNK, KEY_CHUNK),
                                        KEY_CHUNK)]
            for c in range(MAPS):
                u = MAPS * t + c
                s = lax.dot_general(qx_ref[u], kx_ref[cols, :], NT_DIMS,
                                    preferred_element_type=F32) + near
                s_refs[MAPS * (t % 2) + c][:, cols] = s
                m_ref[u] = jnp.maximum(m_ref[u], jnp.maximum(s[:, :LANES], s[:, LANES:]))

    def weighted(t):
        outs = []
        for c in range(MAPS):
            m = jnp.max(m_ref[MAPS * t + c], axis=-1, keepdims=True)
            s_ref = s_refs[MAPS * (t % 2) + c]
            for kc in range(n_kc):
                cols = slice(kc * KEY_CHUNK, (kc + 1) * KEY_CHUNK)
                p_refs[c][:, cols] = jnp.exp2((s_ref[:, cols] - m).astype(BF16))
            acc = _dot(p_refs[c][...], vx_ref[...])
            outs.append(acc[:, :HEAD_DIM] / acc[:, HEAD_DIM:])
        o = outs[0] - lam_ref[0] * outs[1]
        o_ref[0, t * Q_TILE:(t + 1) * Q_TILE, :] = (
            _rms(o, norm_ref[...]) * (1.0 - LAMBDA_INIT)).astype(o_ref.dtype)

    scores(0)
    for t in range(1, Q_TILES_PER_STEP + 1):
        @pl.when(step >= 0)
        def _(t=t):
            if t < Q_TILES_PER_STEP:
                scores(t)
            weighted(t - 1)


def _attention(q, k, v, far, near, lam, head_norm):
    b, s, _ = q.shape
    rows = Q_TILES_PER_STEP * Q_TILE
    smem = pl.BlockSpec(memory_space=pltpu.SMEM)
    return pl.pallas_call(
        _attn_body,
        grid=(b, ATTN_HEADS, s // rows),
        in_specs=[smem,
                  pl.BlockSpec((1, rows, HEAD_DIM), lambda bi, h, i: (bi, i, h)),
                  pl.BlockSpec((1, s, HEAD_DIM), lambda bi, h, i: (bi, 0, h)),
                  pl.BlockSpec((1, s, HEAD_DIM), lambda bi, h, i: (bi, 0, h)),
                  pl.BlockSpec((1, s, HEAD_DIM), lambda bi, h, i: (h, 0, 0)),
                  pl.BlockSpec((1,) + near.shape[1:], lambda bi, h, i: (h, 0, 0)),
                  _resident(head_norm.shape)],
        out_specs=pl.BlockSpec((1, rows, HEAD_DIM), lambda bi, h, i: (bi, i, h)),
        out_shape=jax.ShapeDtypeStruct((b, s, ATTN_WIDTH), BF16),
        scratch_shapes=[pltpu.VMEM((s, 2 * HEAD_DIM), BF16), pltpu.VMEM((s, 2 * HEAD_DIM), BF16),
                        pltpu.VMEM((N_ATTN_UNITS, Q_TILE, 2 * HEAD_DIM), BF16),
                        pltpu.VMEM((N_ATTN_UNITS, Q_TILE, LANES), F32)]
                       + [pltpu.VMEM((Q_TILE, s), F32)] * (2 * MAPS)
                       + [pltpu.VMEM((Q_TILE, s), BF16)] * MAPS,
        compiler_params=_params("parallel", "parallel", "arbitrary"),
        name="attn",
    )(lam, q, k, v, far, near, head_norm)


def _chunk_triangles():
    t = np.arange(RNN_GROUP)[:, None]
    s = np.arange(RNN_GROUP)[None, :]
    same = (t // CHUNK) == (s // CHUNK)
    return np.stack([same & (s <= t), same & (s >= t)]).astype(np.float32)


def _hgrn2_body(q_ref, v_ref, gate_ref, kf_ref, kb_ref, gf_ref, gb_ref, tri_ref, norm_ref, o_ref,
                acc_ref, qd_ref, qm_ref, km_ref, kd_ref, u_ref, dl_ref, sp_ref):
    seq = q_ref.shape[1]
    n_groups = seq // RNN_GROUP
    n_chunks = seq // CHUNK
    per_group = RNN_GROUP // CHUNK
    k_refs = (kf_ref, kb_ref)
    g_refs = (gf_ref, gb_ref)
    mid_row = (CHUNK // 2 - 1, CHUNK // 2)
    last_row = (CHUNK - 1, 0)

    def chunked(t):
        return t.reshape(per_group, CHUNK, HEAD_DIM)

    def flat(t):
        return t.reshape(RNN_GROUP, HEAD_DIM)

    def group_rows(gi):
        return pl.ds(pl.multiple_of(gi * RNN_GROUP, RNN_GROUP), RNN_GROUP)

    def decays(gi):
        rows = group_rows(gi)
        q = chunked(q_ref[0, rows, :].astype(F32))
        for d in range(2):
            log_f = g_refs[d][0, rows, :]
            hi = log_f.astype(BF16)
            lo = (log_f - hi.astype(F32)).astype(BF16)
            both = _dot(tri_ref[d], jnp.concatenate([hi, lo], axis=1))
            cum = chunked(both[:, :HEAD_DIM] + both[:, HEAD_DIM:])
            cum_mid = cum[:, mid_row[d]:mid_row[d] + 1, :]
            cum_last = cum[:, last_row[d]:last_row[d] + 1, :]
            k = chunked(k_refs[d][0, rows, :].astype(F32))
            q_mid = q * jnp.exp(cum - cum_mid)
            k_mid = k * jnp.exp(cum_mid - cum)
            lanes = slice(d * HEAD_DIM, (d + 1) * HEAD_DIM)
            qm_ref[d, rows, :] = flat(q_mid).astype(BF16)
            km_ref[d, rows, :] = flat(k_mid).astype(BF16)
            qd_ref[rows, lanes] = flat(q_mid * jnp.exp(cum_mid)).astype(BF16)
            kd_ref[rows, lanes] = flat(k_mid * jnp.exp(cum_last - cum_mid)).astype(BF16)
            decay = jnp.exp(cum_last)
            for j in range(per_group):
                dl_ref[d, gi * per_group + j] = decay[j]

    def products(gi):
        rows = group_rows(gi)
        v = v_ref[0, rows, :]
        a = None
        for d in range(2):
            a_d = lax.dot_general(qm_ref[d, rows, :], km_ref[d, rows, :], NT_DIMS,
                                  preferred_element_type=F32).astype(BF16)
            a_d = jnp.where(tri_ref[d] > 0, a_d, jnp.zeros_like(a_d))
            a = a_d if a is None else a + a_d
        acc_ref[rows, :] = _dot(a, v)
        k_dec = kd_ref[rows, :]
        for j in range(per_group):
            chunk = slice(j * CHUNK, (j + 1) * CHUNK)
            u = lax.dot_general(v[chunk], k_dec[chunk], TN_DIMS, preferred_element_type=F32)
            for d in range(2):
                u_ref[d, gi * per_group + j] = u[:, d * HEAD_DIM:(d + 1) * HEAD_DIM]

    decays(0)

    def chunk_local(gi, carry):
        products(gi - 1)
        decays(gi)
        return carry
    lax.fori_loop(1, n_groups, chunk_local, 0, unroll=3)
    products(n_groups - 1)

    def scan(i, states):
        new = []
        for d, state in enumerate(states):
            ci = i if d == 0 else n_chunks - 1 - i
            sp_ref[ci, :, d * HEAD_DIM:(d + 1) * HEAD_DIM] = state.astype(BF16)
            new.append(state * dl_ref[d, ci] + u_ref[d, ci])
        return tuple(new)
    zero = jnp.zeros((HEAD_DIM, HEAD_DIM), F32)
    lax.fori_loop(0, n_chunks, scan, (zero, zero))

    def outputs(gi, carry):
        rows = group_rows(gi)
        entering = [lax.dot_general(qd_ref[pl.ds(pl.multiple_of(gi * RNN_GROUP + j * CHUNK, CHUNK), CHUNK), :],
                                    sp_ref[gi * per_group + j], NT_DIMS,
                                    preferred_element_type=F32) for j in range(per_group)]
        total = acc_ref[rows, :] + jnp.concatenate(entering, axis=0)
        gate = gate_ref[0, rows, :].astype(F32)
        o_ref[0, rows, :] = (_rms(total, norm_ref[...]) * gate).astype(o_ref.dtype)
        return carry
    lax.fori_loop(0, n_groups, outputs, 0, unroll=4)


def _hgrn2(q, v, gate, k_fw, k_bw, g_fw, g_bw, head_norm):
    b, s, _ = q.shape
    n_chunks = s // CHUNK
    tri = jnp.asarray(_chunk_triangles(), BF16)
    seq_block = pl.BlockSpec((1, s, HEAD_DIM), lambda bi, h: (bi, 0, h))
    return pl.pallas_call(
        _hgrn2_body,
        grid=(b, RNN_HEADS),
        in_specs=[seq_block] * 7 + [_resident(tri.shape), _resident(head_norm.shape)],
        out_specs=seq_block,
        out_shape=jax.ShapeDtypeStruct((b, s, RNN_WIDTH), BF16),
        scratch_shapes=[pltpu.VMEM((s, HEAD_DIM), F32),
                        pltpu.VMEM((s, 2 * HEAD_DIM), BF16),
                        pltpu.VMEM((2, s, HEAD_DIM), BF16),
                        pltpu.VMEM((2, s, HEAD_DIM), BF16),
                        pltpu.VMEM((s, 2 * HEAD_DIM), BF16),
                        pltpu.VMEM((2, n_chunks, HEAD_DIM, HEAD_DIM), F32),
                        pltpu.VMEM((2, n_chunks, 1, HEAD_DIM), F32),
                        pltpu.VMEM((n_chunks, HEAD_DIM, 2 * HEAD_DIM), BF16)],
        compiler_params=_params("parallel", "parallel"),
        name="hgrn2",
    )(q, v, gate, k_fw, k_bw, g_fw, g_bw, tri, head_norm)


def kernel(x, rel_bias, lb_logits, ffn1_pre_norm, ffn1_w_in, ffn1_w_out, ffn1_post_norm, mix_pre_norm, w_mix_in, lambda_q1, lambda_k1, lambda_q2, lambda_k2, attn_head_norm, rnn_head_norm, w_mix_out, mix_post_norm, ffn2_pre_norm, ffn2_w_in, ffn2_w_out, ffn2_post_norm):
    b, s, d = x.shape
    layer = 0
    row = lambda t: t[layer].reshape(1, -1).astype(F32)

    h1 = _ffn(x.reshape(b * s, d), row(ffn1_pre_norm), ffn1_w_in[layer], ffn1_w_out[layer],
              row(ffn1_post_norm))

    lb = jnp.cumsum(jax.nn.softmax(lb_logits.astype(F32), axis=1), axis=1)[:, layer]
    q_a, k_a, v_a, q_r, i_r, g_r, k_fw, k_bw, g_fw, g_bw = _mixproj(
        h1, row(mix_pre_norm), w_mix_in[layer], lb.reshape(2, 1, RNN_WIDTH))
    seq = lambda t: t.reshape(b, s, t.shape[-1])

    lam = (jnp.exp(jnp.sum(lambda_q1[layer] * lambda_k1[layer]).astype(F32))
           - jnp.exp(jnp.sum(lambda_q2[layer] * lambda_k2[layer]).astype(F32))
           + LAMBDA_INIT).reshape(1)
    far, near = _bias_tables(rel_bias, s)
    o_a = _attention(seq(q_a), seq(k_a), seq(v_a), far, near, lam, row(attn_head_norm))

    o_r = _hgrn2(seq(q_r), seq(i_r), seq(g_r), seq(k_fw), seq(k_bw), seq(g_fw), seq(g_bw),
                 row(rnn_head_norm))

    y = _mix_ffn(h1, o_a.reshape(b * s, ATTN_WIDTH), o_r.reshape(b * s, RNN_WIDTH),
                 w_mix_out[layer], row(mix_post_norm), row(ffn2_pre_norm),
                 ffn2_w_in[layer], ffn2_w_out[layer], row(ffn2_post_norm))
    return y.reshape(b, s, d).astype(x.dtype)
```
